```python
import jax, jax.numpy as jnp
from jax import lax
import numpy as np

D_MODEL = 2048
BATCH = 1
SEQ = 16384
DEPTH = 2
DEC_BATCH = 16
DEC_SEQ = 64
PAST_LEN = 1024

CHUNK = 64
ALPHA = (2 * DEPTH) ** 0.25
BETA = (8 * DEPTH) ** -0.25
LN_EPS = 1e-5
RMS_EPS = 1e-6
FF_DIM = 5504

POOL_WINDOWS = (2, 4, 8, 16)
POOL_DIM = D_MODEL // 4
POOL_GROUP = POOL_DIM // 4
POOL_STATE = 15

NOPE_DIM = 128
ROPE_DIM = 64
V_DIM = 128
MLA_HEADS = (D_MODEL - POOL_DIM) // V_DIM
Q_LORA = D_MODEL // 4
KV_LORA = 512
ROPE_THETA = 10000.0
QUERY_BLOCK = 128
AB_IN = POOL_DIM + Q_LORA + KV_LORA + ROPE_DIM
AB_OUT = POOL_DIM + MLA_HEADS * V_DIM

SSM_INNER = 2 * D_MODEL
SSM_HEAD_DIM = 64
SSM_HEADS = SSM_INNER // SSM_HEAD_DIM
SSM_GROUPS = 8
SSM_HPG = SSM_HEADS // SSM_GROUPS
SSM_STATE = 128
CONV_WIDTH = 4
CONV_DIM = SSM_INNER + 2 * SSM_GROUPS * SSM_STATE
SSD_IN = SSM_INNER + CONV_DIM + SSM_HEADS

kernel_name = 'hybrid_stream_pool_mla_ssd'

F32 = jnp.float32


def layer_norm(h, g, b):
    h32 = h.astype(F32)
    mu = jnp.mean(h32, axis=-1, keepdims=True)
    var = jnp.mean(jnp.square(h32 - mu), axis=-1, keepdims=True)
    return (h32 - mu) * lax.rsqrt(var + LN_EPS) * g.astype(F32) + b.astype(F32)


def post_norm(x, sub, g, b):
    return layer_norm(ALPHA * x + sub, g, b).astype(x.dtype)


def rms_norm(h, g):
    h32 = h.astype(F32)
    h32 = h32 * lax.rsqrt(jnp.mean(jnp.square(h32), axis=-1, keepdims=True) + RMS_EPS)
    return (h32 * g.astype(F32)).astype(h.dtype)


def swiglu(x, w_up, w_down):
    g, u = jnp.split(x @ w_up, 2, axis=-1)
    return (jax.nn.silu(g) * u) @ w_down


def rope(x, pos):
    half = ROPE_DIM // 2
    inv = 1.0 / (ROPE_THETA ** (jnp.arange(half, dtype=F32) * 2.0 / ROPE_DIM))
    ang = pos[:, None] * inv[None, :]
    cos = jnp.cos(ang)[None, :, None, :]
    sin = jnp.sin(ang)[None, :, None, :]
    x1 = x[..., :half].astype(F32)
    x2 = x[..., half:].astype(F32)
    return jnp.concatenate([x1 * cos - x2 * sin, x1 * sin + x2 * cos], axis=-1).astype(x.dtype)


def pool_mixer(u, u_past, pos0, pool_w, pool_scale):
    b, L, _ = u.shape
    full = jnp.concatenate([u_past, u], axis=1)
    cs = jnp.cumsum(full.astype(F32), axis=1)
    cs = jnp.concatenate([jnp.zeros((b, 1, POOL_DIM), F32), cs], axis=1)
    end = cs[:, POOL_STATE + 1:]
    pos = jnp.arange(L, dtype=F32) + pos0
    outs = []
    for g, w in enumerate(POOL_WINDOWS):
        sl = slice(g * POOL_GROUP, (g + 1) * POOL_GROUP)
        start = cs[:, POOL_STATE + 1 - w:POOL_STATE + 1 - w + L, sl]
        cnt = jnp.minimum(pos + 1.0, float(w))[None, :, None]
        outs.append((end[..., sl] - start) / cnt - u[..., sl].astype(F32))
    d = jnp.stack(outs, axis=2).astype(u.dtype)
    y = jnp.einsum('blgc,gcd->blgd', d, pool_w).reshape(b, L, POOL_DIM) * pool_scale
    return y, full[:, -POOL_STATE:]


def block_causal_attention(q, k, v, q_pos, k_pos):
    b, L, H, Dk = q.shape
    qb = min(QUERY_BLOCK, L)
    nb = L // qb
    scale = Dk ** -0.5
    qs = q.reshape(b, nb, qb, H, Dk).swapaxes(0, 1)
    qc = (q_pos // CHUNK).reshape(nb, qb)
    kc = k_pos // CHUNK

    def one(args):
        qblk, qch = args
        s = jnp.einsum('bqhd,bkhd->bhqk', qblk, k, preferred_element_type=F32) * scale
        s = jnp.where(qch[:, None] >= kc[None, :], s, -jnp.inf)
        p = jax.nn.softmax(s, axis=-1)
        return jnp.einsum('bhqk,bkhd->bqhd', p.astype(v.dtype), v)

    o = lax.map(one, (qs, qc))
    return o.swapaxes(0, 1).reshape(b, L, H, v.shape[-1])


def mla_mixer(q_lat, kv_lat, kpe_raw, ckv_past, kpe_past, pos0, q_norm_g, w_q_up, kv_norm_g, w_kv_up):
    b, L, _ = q_lat.shape
    P = ckv_past.shape[1]
    posf = jnp.arange(L, dtype=F32) + pos0
    q = (rms_norm(q_lat, q_norm_g) @ w_q_up).reshape(b, L, MLA_HEADS, NOPE_DIM + ROPE_DIM)
    q = jnp.concatenate([q[..., :NOPE_DIM], rope(q[..., NOPE_DIM:], posf)], axis=-1)
    c_kv = rms_norm(kv_lat, kv_norm_g)
    k_pe = rope(kpe_raw[:, :, None, :], posf)[:, :, 0, :]
    ckv_all = jnp.concatenate([ckv_past, c_kv], axis=1)
    kpe_all = jnp.concatenate([kpe_past, k_pe], axis=1)
    T = P + L
    kv = (ckv_all @ w_kv_up).reshape(b, T, MLA_HEADS, NOPE_DIM + V_DIM)
    k = jnp.concatenate([kv[..., :NOPE_DIM],
                         jnp.broadcast_to(kpe_all[:, :, None, :], (b, T, MLA_HEADS, ROPE_DIM))], axis=-1)
    v = kv[..., NOPE_DIM:]
    q_pos = jnp.arange(L) + pos0
    k_pos = jnp.arange(T) + (pos0 - P)
    o = block_causal_attention(q, k, v, q_pos, k_pos)
    return o.reshape(b, L, MLA_HEADS * V_DIM), c_kv, k_pe


def ab_mixer(x, pool_past, ckv_past, kpe_past, pos0, w_in_ab, pool_w, pool_scale,
             q_norm_g, w_q_up, kv_norm_g, w_kv_up, w_out_ab):
    h = x @ w_in_ab
    u, q_lat, kv_lat, kpe_raw = jnp.split(
        h, [POOL_DIM, POOL_DIM + Q_LORA, POOL_DIM + Q_LORA + KV_LORA], axis=-1)
    y_pool, pool_new = pool_mixer(u, pool_past, pos0, pool_w, pool_scale)
    y_att, ckv_new, kpe_new = mla_mixer(q_lat, kv_lat, kpe_raw, ckv_past, kpe_past, pos0,
                                        q_norm_g, w_q_up, kv_norm_g, w_kv_up)
    out = jnp.concatenate([y_pool.astype(x.dtype), y_att.astype(x.dtype)], axis=-1) @ w_out_ab
    return out, pool_new, ckv_new, kpe_new


def causal_depthwise_conv(xp, w, bias):
    C = xp.shape[-1]
    y = lax.conv_general_dilated(xp, w[:, None, :], window_strides=(1,), padding='VALID',
                                 dimension_numbers=('NWC', 'WIO', 'NWC'), feature_group_count=C)
    return y + bias


def ssd_block(h, blk, A):
    x, dt, Bm, Cm = blk
    L = x.shape[1]
    x32 = x.astype(F32)
    B32 = Bm.astype(F32)
    C32 = Cm.astype(F32)
    acs = jnp.cumsum(dt * A, axis=1)
    acs_t = jnp.moveaxis(acs, 1, -1)
    seg = acs_t[..., :, None] - acs_t[..., None, :]
    causal = jnp.tril(jnp.ones((L, L), dtype=bool))
    decay = jnp.exp(jnp.where(causal, seg, -jnp.inf))
    cb = jnp.einsum('blgn,bsgn->bgls', C32, B32)
    wts = cb[:, :, None] * decay * jnp.moveaxis(dt, 1, -1)[..., None, :]
    y = jnp.einsum('bgrls,bsgrp->blgrp', wts, x32)
    y = y + jnp.einsum('blgn,bgrpn->blgrp', C32, h) * jnp.exp(acs)[..., None]
    to_end = jnp.exp(acs[:, -1:] - acs) * dt
    h_new = h * jnp.exp(acs[:, -1])[..., None, None] + jnp.einsum('blgr,blgn,blgrp->bgrpn', to_end, B32, x32)
    return h_new, y


def ssd_mixer(x, conv_past, ssm_past, w_in_ssd, conv_w, conv_b, dt_bias, a_log, d_skip, ssm_norm_g, w_out_ssd):
    b, L, _ = x.shape
    zxbcdt = x @ w_in_ssd
    z = zxbcdt[..., :SSM_INNER]
    xbc = zxbcdt[..., SSM_INNER:SSM_INNER + CONV_DIM]
    dt = zxbcdt[..., SSM_INNER + CONV_DIM:]
    xbc_all = jnp.concatenate([conv_past, xbc], axis=1)
    conv_new = xbc_all[:, -(CONV_WIDTH - 1):]
    xbc = jax.nn.silu(causal_depthwise_conv(xbc_all, conv_w, conv_b))
    xs = xbc[..., :SSM_INNER].reshape(b, L, SSM_GROUPS, SSM_HPG, SSM_HEAD_DIM)
    Bm = xbc[..., SSM_INNER:SSM_INNER + SSM_GROUPS * SSM_STATE].reshape(b, L, SSM_GROUPS, SSM_STATE)
    Cm = xbc[..., SSM_INNER + SSM_GROUPS * SSM_STATE:].reshape(b, L, SSM_GROUPS, SSM_STATE)
    dt = jax.nn.softplus(dt.astype(F32) + dt_bias.astype(F32)).reshape(b, L, SSM_GROUPS, SSM_HPG)
    A = -jnp.exp(a_log.astype(F32)).reshape(SSM_GROUPS, SSM_HPG)
    h0 = ssm_past.astype(F32).reshape(b, SSM_GROUPS, SSM_HPG, SSM_HEAD_DIM, SSM_STATE)
    if L <= CHUNK:
        h, y = ssd_block(h0, (xs, dt, Bm, Cm), A)
    else:
        nc = L // CHUNK
        to_blocks = lambda t: jnp.moveaxis(t.reshape((b, nc, CHUNK) + t.shape[2:]), 1, 0)
        h, y = lax.scan(lambda c, blk: ssd_block(c, blk, A), h0,
                        (to_blocks(xs), to_blocks(dt), to_blocks(Bm), to_blocks(Cm)))
        y = jnp.moveaxis(y, 0, 1).reshape(b, L, SSM_GROUPS, SSM_HPG, SSM_HEAD_DIM)
    y = y + d_skip.astype(F32).reshape(SSM_GROUPS, SSM_HPG)[..., None] * xs.astype(F32)
    g = (y.reshape(b, L, SSM_INNER) * jax.nn.silu(z.astype(F32))).reshape(b, L, SSM_GROUPS, SSM_INNER // SSM_GROUPS)
    g = g * lax.rsqrt(jnp.mean(jnp.square(g), axis=-1, keepdims=True) + RMS_EPS)
    g = (g.reshape(b, L, SSM_INNER) * ssm_norm_g.astype(F32)).astype(x.dtype)
    h_out = h.reshape(b, SSM_HEADS, SSM_HEAD_DIM, SSM_STATE).astype(ssm_past.dtype)
    return g @ w_out_ssd, conv_new, h_out


def trunk(x, pos0, pool_past, ckv_past, kpe_past, conv_past, ssm_past,
          ffn_pre_up, ffn_pre_down, ffn_post_up, ffn_post_down, ln_g, ln_b,
          w_in_ab, pool_w, pool_scale, q_norm_g, w_q_up, kv_norm_g, w_kv_up, w_out_ab,
          w_in_ssd, conv_w, conv_b, dt_bias, a_log, d_skip, ssm_norm_g, w_out_ssd):
    for layer in range(DEPTH):
        x = post_norm(x, 0.5 * swiglu(x, ffn_pre_up[layer], ffn_pre_down[layer]), ln_g[layer, 0], ln_b[layer, 0])
        if layer % 2 == 0:
            mix, pool_new, ckv_new, kpe_new = ab_mixer(
                x, pool_past, ckv_past, kpe_past, pos0, w_in_ab, pool_w, pool_scale,
                q_norm_g, w_q_up, kv_norm_g, w_kv_up, w_out_ab)
        else:
            mix, conv_new, ssm_new = ssd_mixer(
                x, conv_past, ssm_past, w_in_ssd, conv_w, conv_b, dt_bias, a_log, d_skip, ssm_norm_g, w_out_ssd)
        x = post_norm(x, mix, ln_g[layer, 1], ln_b[layer, 1])
        x = post_norm(x, 0.5 * swiglu(x, ffn_post_up[layer], ffn_post_down[layer]), ln_g[layer, 2], ln_b[layer, 2])
    return x, pool_new, ckv_new, kpe_new, conv_new, ssm_new


def setup_inputs(seed: int = 0) -> dict:
    key = jax.random.key(seed)
    ks = jax.random.split(key, 32)
    nrm = lambda k, shape, s: jax.random.normal(k, shape, F32) * s
    u_dt = jax.random.uniform(ks[28], (SSM_HEADS,), F32)
    dt0 = jnp.maximum(jnp.exp(u_dt * (np.log(0.1) - np.log(0.001)) + np.log(0.001)), 1e-4)
    return {
        'x_prompt': nrm(ks[0], (BATCH, SEQ, D_MODEL), 1.0),
        'x_sample': nrm(ks[1], (DEC_BATCH, DEC_SEQ, D_MODEL), 1.0),
        'state_pool': nrm(ks[2], (DEC_BATCH, POOL_STATE, POOL_DIM), 1.0),
        'cache_ckv': nrm(ks[3], (DEC_BATCH, PAST_LEN, KV_LORA), 1.0),
        'cache_kpe': nrm(ks[4], (DEC_BATCH, PAST_LEN, ROPE_DIM), 1.0),
        'state_conv': nrm(ks[5], (DEC_BATCH, CONV_WIDTH - 1, CONV_DIM), 1.0),
        'state_ssm': nrm(ks[6], (DEC_BATCH, SSM_HEADS, SSM_HEAD_DIM, SSM_STATE), 0.5),
        'ffn_pre_up': nrm(ks[7], (DEPTH, D_MODEL, 2 * FF_DIM), D_MODEL ** -0.5),
        'ffn_pre_down': nrm(ks[8], (DEPTH, FF_DIM, D_MODEL), FF_DIM ** -0.5 * BETA),
        'ffn_post_up': nrm(ks[9], (DEPTH, D_MODEL, 2 * FF_DIM), D_MODEL ** -0.5),
        'ffn_post_down': nrm(ks[10], (DEPTH, FF_DIM, D_MODEL), FF_DIM ** -0.5 * BETA),
        'ln_g': 1.0 + nrm(ks[11], (DEPTH, 3, D_MODEL), 0.02),
        'ln_b': nrm(ks[12], (DEPTH, 3, D_MODEL), 0.02),
        'w_in_ab': nrm(ks[13], (D_MODEL, AB_IN), D_MODEL ** -0.5),
        'pool_w': nrm(ks[14], (len(POOL_WINDOWS), POOL_GROUP, POOL_GROUP), POOL_GROUP ** -0.5),
        'pool_scale': 1.0 + nrm(ks[15], (POOL_DIM,), 0.1),
        'q_norm_g': 1.0 + nrm(ks[16], (Q_LORA,), 0.02),
        'w_q_up': nrm(ks[17], (Q_LORA, MLA_HEADS * (NOPE_DIM + ROPE_DIM)), Q_LORA ** -0.5),
        'kv_norm_g': 1.0 + nrm(ks[18], (KV_LORA,), 0.02),
        'w_kv_up': nrm(ks[19], (KV_LORA, MLA_HEADS * (NOPE_DIM + V_DIM)), KV_LORA ** -0.5),
        'w_out_ab': nrm(ks[20], (AB_OUT, D_MODEL), AB_OUT ** -0.5 * BETA),
        'w_in_ssd': nrm(ks[21], (D_MODEL, SSD_IN), D_MODEL ** -0.5),
        'conv_w': nrm(ks[22], (CONV_WIDTH, CONV_DIM), CONV_WIDTH ** -0.5),
        'conv_b': nrm(ks[23], (CONV_DIM,), 0.02),
        'dt_bias': dt0 + jnp.log(-jnp.expm1(-dt0)),
        'a_log': jnp.log(jax.random.uniform(ks[24], (SSM_HEADS,), F32, 1.0, 16.0)),
        'd_skip': 1.0 + nrm(ks[25], (SSM_HEADS,), 0.1),
        'ssm_norm_g': 1.0 + nrm(ks[26], (SSM_INNER,), 0.02),
        'w_out_ssd': nrm(ks[27], (SSM_INNER, D_MODEL), SSM_INNER ** -0.5 * BETA),
    }


def reference(x_prompt, x_sample, state_pool, cache_ckv, cache_kpe, state_conv, state_ssm,
              ffn_pre_up, ffn_pre_down, ffn_post_up, ffn_post_down, ln_g, ln_b,
              w_in_ab, pool_w, pool_scale, q_norm_g, w_q_up, kv_norm_g, w_kv_up, w_out_ab,
              w_in_ssd, conv_w, conv_b, dt_bias, a_log, d_skip, ssm_norm_g, w_out_ssd):
    W = dict(ffn_pre_up=ffn_pre_up, ffn_pre_down=ffn_pre_down, ffn_post_up=ffn_post_up,
             ffn_post_down=ffn_post_down, ln_g=ln_g, ln_b=ln_b,
             w_in_ab=w_in_ab, pool_w=pool_w, pool_scale=pool_scale, q_norm_g=q_norm_g, w_q_up=w_q_up,
             kv_norm_g=kv_norm_g, w_kv_up=w_kv_up, w_out_ab=w_out_ab,
             w_in_ssd=w_in_ssd, conv_w=conv_w, conv_b=conv_b, dt_bias=dt_bias, a_log=a_log,
             d_skip=d_skip, ssm_norm_g=ssm_norm_g, w_out_ssd=w_out_ssd)
    bp = x_prompt.shape[0]
    dtp = x_prompt.dtype
    y_p, pool_p, ckv_p, kpe_p, conv_p, ssm_p = trunk(
        x_prompt, 0,
        jnp.zeros((bp, POOL_STATE, POOL_DIM), dtp),
        jnp.zeros((bp, 0, KV_LORA), dtp),
        jnp.zeros((bp, 0, ROPE_DIM), dtp),
        jnp.zeros((bp, CONV_WIDTH - 1, CONV_DIM), dtp),
        jnp.zeros((bp, SSM_HEADS, SSM_HEAD_DIM, SSM_STATE), dtp),
        **W)
    y_s, pool_s, ckv_s, kpe_s, conv_s, ssm_s = trunk(
        x_sample, cache_ckv.shape[1], state_pool, cache_ckv, cache_kpe, state_conv, state_ssm, **W)
    return (y_p, y_s, pool_p, pool_s, ckv_p, ckv_s, kpe_p, kpe_s, conv_p, conv_s, ssm_p, ssm_s)
```

```python
import functools

import jax
import jax.numpy as jnp
from jax import lax
from jax.experimental import pallas as pl
from jax.experimental.pallas import tpu as pltpu

F32 = jnp.float32
BF16 = jnp.bfloat16

D_MODEL = 2048
BATCH = 1
SEQ = 16384
DEPTH = 2
DEC_BATCH = 16
DEC_SEQ = 64
PAST_LEN = 1024
CHUNK = 64
ALPHA = (2 * DEPTH) ** 0.25
LN_EPS = 1e-5
RMS_EPS = 1e-6
FF_DIM = 5504
POOL_WINDOWS = (2, 4, 8, 16)
POOL_DIM = 512
POOL_GROUP = 128
POOL_STATE = 15
NOPE_DIM = 128
ROPE_DIM = 64
V_DIM = 128
MLA_HEADS = 12
Q_LORA = 512
KV_LORA = 512
ROPE_THETA = 10000.0
SSM_INNER = 4096
SSM_HEAD_DIM = 64
SSM_HEADS = 64
SSM_GROUPS = 8
SSM_HPG = 8
SSM_STATE = 128
CONV_WIDTH = 4
CONV_DIM = SSM_INNER + 2 * SSM_GROUPS * SSM_STATE

NP = BATCH * SEQ
NS = DEC_BATCH * DEC_SEQ
NTOK = NP + NS

LANES = 128
SUBLANES = 8
VMEM_LIMIT_BYTES = 56 * 1024 * 1024

FF_PAD = 5632
FF_TILE = 512
FFN_TM = 512
TOK_TM = 512
ATT_TQ = 512
ATT_TK = 512
SSD_L = 128
QK_PAD = 256
SSD_IN_PAD = SSM_INNER + CONV_DIM + LANES
SSD_IN_TN = 1152
POOL_EXT = 16
ATT_SCALE = float((NOPE_DIM + ROPE_DIM) ** -0.5)


def _cparams(sem):
    return pltpu.CompilerParams(dimension_semantics=sem, vmem_limit_bytes=VMEM_LIMIT_BYTES)


def _layer_norm(y, g, b):
    mu = jnp.mean(y, axis=-1, keepdims=True)
    d = y - mu
    var = jnp.mean(d * d, axis=-1, keepdims=True)
    return d * lax.rsqrt(var + LN_EPS) * g + b


def _rms(h, g):
    return h * lax.rsqrt(jnp.mean(h * h, axis=-1, keepdims=True) + RMS_EPS) * g


def _silu(x):
    return x * jax.nn.sigmoid(x)


def _dot(a, b):
    return jnp.dot(a, b, preferred_element_type=F32)


def _dot_nt(a, b):
    return lax.dot_general(a, b, (((1,), (1,)), ((), ())), preferred_element_type=F32)


def _dot_tn(a, b):
    return lax.dot_general(a, b, (((0,), (0,)), ((), ())), preferred_element_type=F32)


def _split2(v):
    hi = v.astype(BF16)
    lo = (v - hi.astype(F32)).astype(BF16)
    return hi, lo


def _split3(v):
    hi = v.astype(BF16)
    r = v - hi.astype(F32)
    mid = r.astype(BF16)
    lo = (r - mid.astype(F32)).astype(BF16)
    return hi, mid, lo


def _ffn_kernel(x_ref, wg_ref, wu_ref, wd_ref, g_ref, b_ref, o_ref, xb_ref):
    j = pl.program_id(1)

    @pl.when(j == 0)
    def _():
        xb_ref[...] = x_ref[...].astype(BF16)
        o_ref[...] = jnp.zeros_like(o_ref)

    xb = xb_ref[...]
    h = (_silu(_dot(xb, wg_ref[...])) * _dot(xb, wu_ref[...])).astype(BF16)
    o_ref[...] += _dot(h, wd_ref[...])

    @pl.when(j == pl.num_programs(1) - 1)
    def _():
        o_ref[...] = _layer_norm(ALPHA * x_ref[...] + 0.5 * o_ref[...], g_ref[...], b_ref[...])


def _ffn(x, w_up, w_down, g, b):
    pad = FF_PAD - FF_DIM
    wg = jnp.pad(w_up[:, :FF_DIM].astype(BF16), ((0, 0), (0, pad)))
    wu = jnp.pad(w_up[:, FF_DIM:].astype(BF16), ((0, 0), (0, pad)))
    wd = jnp.pad(w_down.astype(BF16), ((0, pad), (0, 0)))
    n = x.shape[0]
    return pl.pallas_call(
        _ffn_kernel,
        grid=(n // FFN_TM, FF_PAD // FF_TILE),
        in_specs=[
            pl.BlockSpec((FFN_TM, D_MODEL), lambda i, j: (i, 0)),
            pl.BlockSpec((D_MODEL, FF_TILE), lambda i, j: (0, j)),
            pl.BlockSpec((D_MODEL, FF_TILE), lambda i, j: (0, j)),
            pl.BlockSpec((FF_TILE, D_MODEL), lambda i, j: (j, 0)),
            pl.BlockSpec((1, D_MODEL), lambda i, j: (0, 0)),
            pl.BlockSpec((1, D_MODEL), lambda i, j: (0, 0)),
        ],
        out_specs=pl.BlockSpec((FFN_TM, D_MODEL), lambda i, j: (i, 0)),
        out_shape=jax.ShapeDtypeStruct((n, D_MODEL), F32),
        scratch_shapes=[pltpu.VMEM((FFN_TM, D_MODEL), BF16)],
        compiler_params=_cparams(("parallel", "arbitrary")),
        name="ffn_postnorm",
    )(x, wg, wu, wd, g.reshape(1, -1), b.reshape(1, -1))


def _pool_window_mean_minus(ext_ref, base, rows, pos0):
    first = base + POOL_EXT
    t = lax.broadcasted_iota(jnp.int32, (rows, POOL_GROUP), 0)
    posp1 = (pos0 + t + 1).astype(F32)
    outs = []
    for g, w in enumerate(POOL_WINDOWS):
        cols = slice(g * POOL_GROUP, (g + 1) * POOL_GROUP)
        cur = ext_ref[pl.ds(first, rows), cols]
        tot = cur
        for s in range(1, w):
            tot = tot + ext_ref[pl.ds(first - s, rows), cols]
        outs.append(tot / jnp.minimum(posp1, float(w)) - cur)
    return outs


def _ab_in_kernel(n_prompt_tiles, x_ref, w_ref, past_ref, pw_ref, ps_ref, qg_ref, kg_ref, rope_ref,
                  u_ref, yp_ref, qn_ref, ckv_ref, kpe_ref, ext_ref):
    i = pl.program_id(0)
    tm = x_ref.shape[0]
    h = _dot(x_ref[...].astype(BF16), w_ref[...])
    u = h[:, :POOL_DIM]
    u_ref[...] = u
    qn_ref[...] = _rms(h[:, POOL_DIM:POOL_DIM + Q_LORA], qg_ref[...]).astype(BF16)
    ckv_ref[...] = _rms(h[:, POOL_DIM + Q_LORA:POOL_DIM + Q_LORA + KV_LORA], kg_ref[...])
    kv = h[:, POOL_DIM + Q_LORA + KV_LORA:] * rope_ref[...]
    lane = lax.broadcasted_iota(jnp.int32, kv.shape, 1)
    kpe_ref[...] = jnp.where(lane < ROPE_DIM, kv + pltpu.roll(kv, ROPE_DIM, axis=1), 0.0)

    def finish(d_groups):
        for g in range(len(POOL_WINDOWS)):
            cols = slice(g * POOL_GROUP, (g + 1) * POOL_GROUP)
            y = _dot(d_groups[g].astype(BF16), pw_ref[g]) * ps_ref[:, cols]
            yp_ref[:, cols] = y.astype(BF16)

    @pl.when(i < n_prompt_tiles)
    def _():
        @pl.when(i == 0)
        def _():
            ext_ref[pl.ds(0, POOL_EXT), :] = jnp.zeros((POOL_EXT, POOL_DIM), F32)

        ext_ref[pl.ds(POOL_EXT, tm), :] = u
        finish(_pool_window_mean_minus(ext_ref, 0, tm, i * tm))
        ext_ref[pl.ds(0, POOL_EXT), :] = u[tm - POOL_EXT:, :]

    @pl.when(i >= n_prompt_tiles)
    def _():
        nseg = tm // DEC_SEQ
        stride = POOL_EXT + DEC_SEQ
        parts = [[] for _ in POOL_WINDOWS]
        for s in range(nseg):
            ext_ref[pl.ds(s * stride, POOL_EXT), :] = past_ref[s]
            ext_ref[pl.ds(s * stride + POOL_EXT, DEC_SEQ), :] = u[s * DEC_SEQ:(s + 1) * DEC_SEQ, :]
        for s in range(nseg):
            d = _pool_window_mean_minus(ext_ref, s * stride, DEC_SEQ, PAST_LEN)
            for g in range(len(POOL_WINDOWS)):
                parts[g].append(d[g])
        finish([jnp.concatenate(p, axis=0) for p in parts])


def _ab_in(x, w_in_x, pool_past, pool_w, pool_scale, q_norm_g, kv_norm_g, rope_k):
    n = x.shape[0]
    tm = TOK_TM
    nseg = tm // DEC_SEQ
    npt = NP // tm
    wcols = w_in_x.shape[1]
    ext_rows = max(tm + POOL_EXT, nseg * (POOL_EXT + DEC_SEQ))
    const = lambda i: (0, 0)
    return pl.pallas_call(
        functools.partial(_ab_in_kernel, npt),
        grid=(n // tm,),
        in_specs=[
            pl.BlockSpec((tm, D_MODEL), lambda i: (i, 0)),
            pl.BlockSpec((D_MODEL, wcols), const),
            pl.BlockSpec((nseg, POOL_EXT, POOL_DIM), lambda i: (jnp.maximum(i - npt, 0), 0, 0)),
            pl.BlockSpec((len(POOL_WINDOWS), POOL_GROUP, POOL_GROUP), lambda i: (0, 0, 0)),
            pl.BlockSpec((1, POOL_DIM), const),
            pl.BlockSpec((1, Q_LORA), const),
            pl.BlockSpec((1, KV_LORA), const),
            pl.BlockSpec((tm, LANES), lambda i: (i, 0)),
        ],
        out_specs=[
            pl.BlockSpec((tm, POOL_DIM), lambda i: (i, 0)),
            pl.BlockSpec((tm, POOL_DIM), lambda i: (i, 0)),
            pl.BlockSpec((tm, Q_LORA), lambda i: (i, 0)),
            pl.BlockSpec((tm, KV_LORA), lambda i: (i, 0)),
            pl.BlockSpec((tm, LANES), lambda i: (i, 0)),
        ],
        out_shape=[
            jax.ShapeDtypeStruct((n, POOL_DIM), F32),
            jax.ShapeDtypeStruct((n, POOL_DIM), BF16),
            jax.ShapeDtypeStruct((n, Q_LORA), BF16),
            jax.ShapeDtypeStruct((n, KV_LORA), F32),
            jax.ShapeDtypeStruct((n, LANES), F32),
        ],
        scratch_shapes=[pltpu.VMEM((ext_rows, POOL_DIM), F32)],
        compiler_params=_cparams(("arbitrary",)),
        name="ab_in",
    )(x, w_in_x, pool_past, pool_w.astype(BF16), pool_scale.reshape(1, -1),
      q_norm_g.reshape(1, -1), kv_norm_g.reshape(1, -1), rope_k)


def _q_up_kernel(qn_ref, w_ref, rope_ref, q_ref):
    qn = qn_ref[...]
    rope = rope_ref[...]
    lane = lax.broadcasted_iota(jnp.int32, rope.shape, 1)
    hpc = 4
    ncol = hpc * LANES
    for c in range(MLA_HEADS // hpc):
        nope = _dot(qn, w_ref[:, c * ncol:(c + 1) * ncol])
        rp = _dot(qn, w_ref[:, MLA_HEADS * LANES + c * ncol:MLA_HEADS * LANES + (c + 1) * ncol])
        for hh in range(hpc):
            hd = c * hpc + hh
            v = rp[:, hh * LANES:(hh + 1) * LANES] * rope
            rot = jnp.where(lane < ROPE_DIM, v + pltpu.roll(v, ROPE_DIM, axis=1), 0.0)
            q_ref[hd, :, 0:LANES] = (nope[:, hh * LANES:(hh + 1) * LANES] * ATT_SCALE).astype(BF16)
            q_ref[hd, :, LANES:QK_PAD] = (rot * ATT_SCALE).astype(BF16)


def _q_up(qn, wq_x, rope_q):
    n = qn.shape[0]
    tm = TOK_TM
    return pl.pallas_call(
        _q_up_kernel,
        grid=(n // tm,),
        in_specs=[
            pl.BlockSpec((tm, Q_LORA), lambda i: (i, 0)),
            pl.BlockSpec(wq_x.shape, lambda i: (0, 0)),
            pl.BlockSpec((tm, LANES), lambda i: (i, 0)),
        ],
        out_specs=pl.BlockSpec((MLA_HEADS, tm, QK_PAD), lambda i: (0, i, 0)),
        out_shape=jax.ShapeDtypeStruct((MLA_HEADS, n, QK_PAD), BF16),
        compiler_params=_cparams(("parallel",)),
        name="q_up",
    )(qn, wq_x, rope_q)


def _kv_up_kernel(ckv_ref, kpe_ref, w_ref, k_ref, v_ref):
    c = ckv_ref[...].astype(BF16)
    kpe = kpe_ref[...].astype(BF16)
    hpc = 4
    ncol = hpc * LANES
    for j in range(MLA_HEADS // hpc):
        kn = _dot(c, w_ref[:, j * ncol:(j + 1) * ncol])
        vv = _dot(c, w_ref[:, MLA_HEADS * LANES + j * ncol:MLA_HEADS * LANES + (j + 1) * ncol])
        for hh in range(hpc):
            hd = j * hpc + hh
            k_ref[hd, :, 0:LANES] = kn[:, hh * LANES:(hh + 1) * LANES].astype(BF16)
            k_ref[hd, :, LANES:QK_PAD] = kpe
            v_ref[hd] = vv[:, hh * LANES:(hh + 1) * LANES].astype(BF16)


def _kv_up(ckv, kpe128, wkv_x):
    n = ckv.shape[0]
    tm = TOK_TM
    return pl.pallas_call(
        _kv_up_kernel,
        grid=(n // tm,),
        in_specs=[
            pl.BlockSpec((tm, KV_LORA), lambda i: (i, 0)),
            pl.BlockSpec((tm, LANES), lambda i: (i, 0)),
            pl.BlockSpec(wkv_x.shape, lambda i: (0, 0)),
        ],
        out_specs=[
            pl.BlockSpec((MLA_HEADS, tm, QK_PAD), lambda i: (0, i, 0)),
            pl.BlockSpec((MLA_HEADS, tm, V_DIM), lambda i: (0, i, 0)),
        ],
        out_shape=[
            jax.ShapeDtypeStruct((MLA_HEADS, n, QK_PAD), BF16),
            jax.ShapeDtypeStruct((MLA_HEADS, n, V_DIM), BF16),
        ],
        compiler_params=_cparams(("parallel",)),
        name="kv_up",
    )(ckv, kpe128, wkv_x)


def _attn_prompt_kernel(q_ref, k_ref, v_ref, o_ref):
    i = pl.program_id(1)
    tq = q_ref.shape[1]
    tk = ATT_TK
    q = q_ref[0]

    def step(c, carry, mask):
        m, l, acc = carry
        start = pl.multiple_of(c * tk, tk)
        s = _dot_nt(q, k_ref[0, pl.ds(start, tk), :])
        if mask is not None:
            s = jnp.where(mask, s, -jnp.inf)
        m_new = jnp.maximum(m, jnp.max(s, axis=1, keepdims=True))
        alpha = jnp.exp(m - m_new)
        p = jnp.exp(s - m_new)
        l = alpha * l + jnp.sum(p, axis=1, keepdims=True)
        acc = alpha * acc + _dot(p.astype(BF16), v_ref[0, pl.ds(start, tk), :])
        return m_new, l, acc

    carry = (jnp.full((tq, 1), -jnp.inf, F32), jnp.zeros((tq, 1), F32), jnp.zeros((tq, V_DIM), F32))
    nfull = i * (tq // tk)
    carry = lax.fori_loop(0, nfull, lambda c, cr: step(c, cr, None), carry)
    r = lax.broadcasted_iota(jnp.int32, (tq, tk), 0) // CHUNK
    cc = lax.broadcasted_iota(jnp.int32, (tq, tk), 1) // CHUNK
    for d in range(tq // tk):
        carry = step(nfull + d, carry, r >= cc + d * (tk // CHUNK))
    _, l, acc = carry
    o_ref[...] = (acc / l).astype(BF16)


def _attn_prompt(q, k, v):
    tq = ATT_TQ
    return pl.pallas_call(
        _attn_prompt_kernel,
        grid=(MLA_HEADS, NP // tq),
        in_specs=[
            pl.BlockSpec((1, tq, QK_PAD), lambda h, i: (h, i, 0)),
            pl.BlockSpec((1, NP, QK_PAD), lambda h, i: (h, 0, 0)),
            pl.BlockSpec((1, NP, V_DIM), lambda h, i: (h, 0, 0)),
        ],
        out_specs=pl.BlockSpec((tq, V_DIM), lambda h, i: (i, h)),
        out_shape=jax.ShapeDtypeStruct((NP, MLA_HEADS * V_DIM), BF16),
        compiler_params=_cparams(("parallel", "arbitrary")),
        name="attn_prompt",
    )(q, k, v)


def _attn_sample_kernel(q_ref, kp_ref, vp_ref, kn_ref, vn_ref, o_ref):
    for hd in range(MLA_HEADS):
        q = q_ref[hd]
        sp = _dot_nt(q, kp_ref[hd])
        sn = _dot_nt(q, kn_ref[hd])
        m = jnp.maximum(jnp.max(sp, axis=1, keepdims=True), jnp.max(sn, axis=1, keepdims=True))
        pp = jnp.exp(sp - m)
        pn = jnp.exp(sn - m)
        l = jnp.sum(pp, axis=1, keepdims=True) + jnp.sum(pn, axis=1, keepdims=True)
        o = _dot(pp.astype(BF16), vp_ref[hd]) + _dot(pn.astype(BF16), vn_ref[hd])
        o_ref[:, hd * V_DIM:(hd + 1) * V_DIM] = (o / l).astype(BF16)


def _attn_sample(q, k_past, v_past, k_new, v_new):
    assert PAST_LEN % CHUNK == 0 and DEC_SEQ <= CHUNK
    first = NP // DEC_SEQ
    return pl.pallas_call(
        _attn_sample_kernel,
        grid=(DEC_BATCH,),
        in_specs=[
            pl.BlockSpec((MLA_HEADS, DEC_SEQ, QK_PAD), lambda b: (0, first + b, 0)),
            pl.BlockSpec((MLA_HEADS, PAST_LEN, QK_PAD), lambda b: (0, b, 0)),
            pl.BlockSpec((MLA_HEADS, PAST_LEN, V_DIM), lambda b: (0, b, 0)),
            pl.BlockSpec((MLA_HEADS, DEC_SEQ, QK_PAD), lambda b: (0, first + b, 0)),
            pl.BlockSpec((MLA_HEADS, DEC_SEQ, V_DIM), lambda b: (0, first + b, 0)),
        ],
        out_specs=pl.BlockSpec((DEC_SEQ, MLA_HEADS * V_DIM), lambda b: (b, 0)),
        out_shape=jax.ShapeDtypeStruct((NS, MLA_HEADS * V_DIM), BF16),
        compiler_params=_cparams(("parallel",)),
        name="attn_sample",
    )(q, k_past, v_past, k_new, v_new)


def _ab_out_kernel(npt, x_ref, yp_ref, ap_ref, as_ref, wp_ref, wa_ref, g_ref, b_ref, o_ref):
    i = pl.program_id(0)
    att = jnp.where(i < npt, ap_ref[...], as_ref[...])
    mix = _dot(yp_ref[...], wp_ref[...]) + _dot(att, wa_ref[...])
    o_ref[...] = _layer_norm(ALPHA * x_ref[...] + mix, g_ref[...], b_ref[...])


def _ab_out(x, ypool, att_p, att_s, w_out, g, b):
    n = x.shape[0]
    tm = TOK_TM
    npt = NP // tm
    w = w_out.astype(BF16)
    adim = MLA_HEADS * V_DIM
    const = lambda i: (0, 0)
    return pl.pallas_call(
        functools.partial(_ab_out_kernel, npt),
        grid=(n // tm,),
        in_specs=[
            pl.BlockSpec((tm, D_MODEL), lambda i: (i, 0)),
            pl.BlockSpec((tm, POOL_DIM), lambda i: (i, 0)),
            pl.BlockSpec((tm, adim), lambda i: (jnp.minimum(i, npt - 1), 0)),
            pl.BlockSpec((tm, adim), lambda i: (jnp.maximum(i - npt, 0), 0)),
            pl.BlockSpec((POOL_DIM, D_MODEL), const),
            pl.BlockSpec((adim, D_MODEL), const),
            pl.BlockSpec((1, D_MODEL), const),
            pl.BlockSpec((1, D_MODEL), const),
        ],
        out_specs=pl.BlockSpec((tm, D_MODEL), lambda i: (i, 0)),
        out_shape=jax.ShapeDtypeStruct((n, D_MODEL), F32),
        compiler_params=_cparams(("parallel",)),
        name="ab_out",
    )(x, ypool, att_p, att_s, w[:POOL_DIM], w[POOL_DIM:], g.reshape(1, -1), b.reshape(1, -1))


def _ssd_out_kernel(x_ref, a_ref, w_ref, g_ref, b_ref, o_ref):
    o_ref[...] = _layer_norm(ALPHA * x_ref[...] + _dot(a_ref[...], w_ref[...]), g_ref[...], b_ref[...])


def _ssd_out(x, act, w_out, g, b):
    n = x.shape[0]
    tm = TOK_TM
    const = lambda i: (0, 0)
    return pl.pallas_call(
        _ssd_out_kernel,
        grid=(n // tm,),
        in_specs=[
            pl.BlockSpec((tm, D_MODEL), lambda i: (i, 0)),
            pl.BlockSpec((tm, SSM_INNER), lambda i: (i, 0)),
            pl.BlockSpec((SSM_INNER, D_MODEL), const),
            pl.BlockSpec((1, D_MODEL), const),
            pl.BlockSpec((1, D_MODEL), const),
        ],
        out_specs=pl.BlockSpec((tm, D_MODEL), lambda i: (i, 0)),
        out_shape=jax.ShapeDtypeStruct((n, D_MODEL), F32),
        compiler_params=_cparams(("parallel",)),
        name="ssd_out",
    )(x, act, w_out.astype(BF16), g.reshape(1, -1), b.reshape(1, -1))


def _ssd_in_kernel(x_ref, w_ref, o_ref, xb_ref):
    @pl.when(pl.program_id(1) == 0)
    def _():
        xb_ref[...] = x_ref[...].astype(BF16)

    o_ref[...] = _dot(xb_ref[...], w_ref[...])


def _ssd_in(x, w_x):
    n = x.shape[0]
    tm = 1024
    return pl.pallas_call(
        _ssd_in_kernel,
        grid=(n // tm, SSD_IN_PAD // SSD_IN_TN),
        in_specs=[
            pl.BlockSpec((tm, D_MODEL), lambda i, j: (i, 0)),
            pl.BlockSpec((D_MODEL, SSD_IN_TN), lambda i, j: (0, j)),
        ],
        out_specs=pl.BlockSpec((tm, SSD_IN_TN), lambda i, j: (i, j)),
        out_shape=jax.ShapeDtypeStruct((n, SSD_IN_PAD), F32),
        scratch_shapes=[pltpu.VMEM((tm, D_MODEL), BF16)],
        compiler_params=_cparams(("parallel", "arbitrary")),
        name="ssd_in",
    )(x, w_x)


def _ssd_kernel(rows, fresh_each_step, z_ref, xs_ref, bc_ref, dt_ref, cpast_ref, h0_ref, cw_ref,
                cb_ref, dtb_ref, a_ref, dsk_ref, ng_ref, e_ref, y_ref, h_ref, ext_ref):
    c = pl.program_id(0)
    L = SSD_L
    G, R, P, S = SSM_GROUPS, SSM_HPG, SSM_HEAD_DIM, SSM_STATE
    halo = SUBLANES

    def start_sequence():
        ext_ref[pl.ds(0, halo), :] = cpast_ref[0]
        h_ref[0] = h0_ref[0]

    if fresh_each_step:
        start_sequence()
    else:
        pl.when(c == 0)(start_sequence)

    ext_ref[pl.ds(halo, rows), 0:SSM_INNER] = xs_ref[...]
    ext_ref[pl.ds(halo, rows), SSM_INNER:CONV_DIM] = bc_ref[...]
    if rows < L:
        ext_ref[pl.ds(halo + rows, L - rows), :] = jnp.zeros((L - rows, CONV_DIM), F32)

    conv = cb_ref[...] + cw_ref[0:1, :] * ext_ref[pl.ds(halo - 3, L), :]
    for k in range(1, CONV_WIDTH):
        conv = conv + cw_ref[k:k + 1, :] * ext_ref[pl.ds(halo - 3 + k, L), :]
    ext_ref[pl.ds(0, halo), :] = ext_ref[pl.ds(rows, halo), :]
    act = _silu(conv)
    xs = act[:, :SSM_INNER]
    xsb = xs.astype(BF16)
    bmat = act[:, SSM_INNER:SSM_INNER + G * S].astype(BF16)
    cmat = act[:, SSM_INNER + G * S:].astype(BF16)

    dtr = dt_ref[...] + dtb_ref[...]
    dt = jnp.maximum(dtr, 0.0) + jnp.log1p(jnp.exp(-jnp.abs(dtr)))
    if rows < L:
        dt = jnp.concatenate([dt, jnp.zeros((L - rows, LANES), F32)], axis=0)
    da = dt * a_ref[...]
    ti = lax.broadcasted_iota(jnp.int32, (L, L), 0)
    si = lax.broadcasted_iota(jnp.int32, (L, L), 1)
    causal = ti >= si
    tri = jnp.where(causal, 1.0, 0.0).astype(BF16)
    acs = sum(_dot(tri, part) for part in _split3(da))
    acs_t = acs.T
    dt_t = dt.T
    last = acs[L - 1:L, :]
    to_end = jnp.exp(last - acs) * dt
    eacs = jnp.exp(acs)
    onehot = e_ref[...]
    expand = lambda v: sum(_dot(part, onehot) for part in _split2(v))
    x_end = (xs * expand(to_end)).astype(BF16)
    eacs_x = expand(eacs)
    dec_x = sum(_dot(part, onehot) for part in _split3(jnp.broadcast_to(jnp.exp(last), (SUBLANES, LANES))))[0:1, :]

    lane4 = lax.broadcasted_iota(jnp.int32, (L, 4 * P), 1) // P
    for g in range(G):
        bg = bmat[:, g * S:(g + 1) * S]
        cg = cmat[:, g * S:(g + 1) * S]
        cbm = _dot_nt(cg, bg)
        cols = slice(g * R * P, (g + 1) * R * P)
        ht = h_ref[0, :, cols]
        y_state = _dot(cg, ht.astype(BF16)) * eacs_x[:, cols]
        y_parts = []
        for half in range(R // 4):
            x4 = xsb[:, g * R * P + half * 4 * P:g * R * P + (half + 1) * 4 * P]
            x_bd = jnp.concatenate([jnp.where(lane4 == r, x4, jnp.zeros_like(x4)) for r in range(4)], axis=0)
            wts = []
            for r in range(4):
                hd = g * R + half * 4 + r
                seg = acs[:, hd:hd + 1] - acs_t[hd:hd + 1, :]
                decay = jnp.exp(jnp.where(causal, seg, -jnp.inf))
                wts.append((cbm * decay * dt_t[hd:hd + 1, :]).astype(BF16))
            y_parts.append(_dot(jnp.concatenate(wts, axis=1), x_bd))
        y = jnp.concatenate(y_parts, axis=1) + y_state + dsk_ref[:, cols] * xs[:, cols]
        gz = y[:rows] * _silu(z_ref[:, cols])
        gz = gz * lax.rsqrt(jnp.mean(gz * gz, axis=-1, keepdims=True) + RMS_EPS)
        y_ref[:, cols] = (gz * ng_ref[:, cols]).astype(BF16)
        h_ref[0, :, cols] = ht * dec_x[:, cols] + _dot_tn(bg, x_end[:, cols])


def _ssd_core(zx, row0, nblk, rows, fresh_each_step, conv_past8, h0_t, conv_w8, conv_b, dt_bias, a_neg,
              d_skip_x, norm_g, onehot):
    L = SSD_L
    r0 = row0 // rows
    nseq = nblk if fresh_each_step else 1
    seq = (lambda c: c) if fresh_each_step else (lambda c: 0)
    const = lambda c: (0, 0)
    zi = SSM_INNER // LANES
    return pl.pallas_call(
        functools.partial(_ssd_kernel, rows, fresh_each_step),
        grid=(nblk,),
        in_specs=[
            pl.BlockSpec((rows, SSM_INNER), lambda c: (r0 + c, 0)),
            pl.BlockSpec((rows, SSM_INNER), lambda c: (r0 + c, 1)),
            pl.BlockSpec((rows, 2 * SSM_GROUPS * SSM_STATE), lambda c: (r0 + c, 4)),
            pl.BlockSpec((rows, LANES), lambda c: (r0 + c, (SSM_INNER + CONV_DIM) // LANES)),
            pl.BlockSpec((1, SUBLANES, CONV_DIM), lambda c: (seq(c), 0, 0)),
            pl.BlockSpec((1, SSM_STATE, SSM_INNER), lambda c: (seq(c), 0, 0)),
            pl.BlockSpec((SUBLANES, CONV_DIM), const),
            pl.BlockSpec((1, CONV_DIM), const),
            pl.BlockSpec((1, LANES), const),
            pl.BlockSpec((1, LANES), const),
            pl.BlockSpec((1, SSM_INNER), const),
            pl.BlockSpec((1, SSM_INNER), const),
            pl.BlockSpec((LANES, SSM_INNER), const),
        ],
        out_specs=[
            pl.BlockSpec((rows, SSM_INNER), lambda c: (c, 0)),
            pl.BlockSpec((1, SSM_STATE, SSM_INNER), lambda c: (seq(c), 0, 0)),
        ],
        out_shape=[
            jax.ShapeDtypeStruct((nblk * rows, SSM_INNER), BF16),
            jax.ShapeDtypeStruct((nseq, SSM_STATE, SSM_INNER), F32),
        ],
        scratch_shapes=[pltpu.VMEM((L + 2 * SUBLANES, CONV_DIM), F32)],
        compiler_params=_cparams(("arbitrary",)),
        name="ssd_core_seq" if not fresh_each_step else "ssd_core_blocks",
    )(zx, zx, zx, zx, conv_past8, h0_t, conv_w8, conv_b, dt_bias, a_neg, d_skip_x, norm_g, onehot)


def _rope_table():
    half = ROPE_DIM // 2
    inv = 1.0 / (ROPE_THETA ** (jnp.arange(half, dtype=F32) * 2.0 / ROPE_DIM))
    pos = jnp.concatenate([
        jnp.tile(jnp.arange(SEQ, dtype=F32), BATCH),
        jnp.tile(jnp.arange(DEC_SEQ, dtype=F32) + PAST_LEN, DEC_BATCH)])
    ang = pos[:, None] * inv[None, :]
    cos, sin = jnp.cos(ang), jnp.sin(ang)
    return jnp.concatenate([cos, cos, -sin, sin], axis=1)


def _swap_halves(w):
    half = w.shape[-1] // 2
    return jnp.concatenate([w[..., half:], w[..., :half]], axis=-1)


def _prep_ab_weights(w_in_ab, w_q_up, w_kv_up):
    kpe_w = w_in_ab[:, POOL_DIM + Q_LORA + KV_LORA:]
    w_in_x = jnp.concatenate([w_in_ab, _swap_halves(kpe_w)], axis=1).astype(BF16)
    wq = w_q_up.reshape(Q_LORA, MLA_HEADS, NOPE_DIM + ROPE_DIM)
    wq_rope = wq[..., NOPE_DIM:]
    wq_x = jnp.concatenate([
        wq[..., :NOPE_DIM].reshape(Q_LORA, -1),
        jnp.concatenate([wq_rope, _swap_halves(wq_rope)], axis=-1).reshape(Q_LORA, -1)], axis=1).astype(BF16)
    wkv = w_kv_up.reshape(KV_LORA, MLA_HEADS, NOPE_DIM + V_DIM)
    wkv_x = jnp.concatenate([wkv[..., :NOPE_DIM].reshape(KV_LORA, -1),
                             wkv[..., NOPE_DIM:].reshape(KV_LORA, -1)], axis=1).astype(BF16)
    return w_in_x, wq_x, wkv_x


def _state_to_t(h):
    b = h.shape[0]
    return jnp.transpose(h, (0, 3, 1, 2)).reshape(b, SSM_STATE, SSM_INNER)


def _state_from_t(ht):
    b = ht.shape[0]
    return jnp.transpose(ht.reshape(b, SSM_STATE, SSM_HEADS, SSM_HEAD_DIM), (0, 2, 3, 1))


def kernel(x_prompt, x_sample, state_pool, cache_ckv, cache_kpe, state_conv, state_ssm, ffn_pre_up, ffn_pre_down, ffn_post_up, ffn_post_down, ln_g, ln_b, w_in_ab, pool_w, pool_scale, q_norm_g, w_q_up, kv_norm_g, w_kv_up, w_out_ab, w_in_ssd, conv_w, conv_b, dt_bias, a_log, d_skip, ssm_norm_g, w_out_ssd):
    x = jnp.concatenate([x_prompt.reshape(NP, D_MODEL), x_sample.reshape(NS, D_MODEL)], axis=0)

    x = _ffn(x, ffn_pre_up[0], ffn_pre_down[0], ln_g[0, 0], ln_b[0, 0])
    rope = _rope_table()
    w_in_x, wq_x, wkv_x = _prep_ab_weights(w_in_ab, w_q_up, w_kv_up)
    pool_past = jnp.pad(state_pool, ((0, 0), (POOL_EXT - POOL_STATE, 0), (0, 0)))
    u, ypool, qn, ckv, kpe128 = _ab_in(x, w_in_x, pool_past, pool_w, pool_scale, q_norm_g, kv_norm_g, rope)
    q = _q_up(qn, wq_x, rope)
    k_new, v_new = _kv_up(ckv, kpe128, wkv_x)
    k_past, v_past = _kv_up(cache_ckv.reshape(DEC_BATCH * PAST_LEN, KV_LORA),
                            jnp.pad(cache_kpe.reshape(DEC_BATCH * PAST_LEN, ROPE_DIM), ((0, 0), (0, LANES - ROPE_DIM))),
                            wkv_x)
    att_p = _attn_prompt(q, k_new, v_new)
    att_s = _attn_sample(q, k_past, v_past, k_new, v_new)
    x = _ab_out(x, ypool, att_p, att_s, w_out_ab, ln_g[0, 1], ln_b[0, 1])
    x = _ffn(x, ffn_post_up[0], ffn_post_down[0], ln_g[0, 2], ln_b[0, 2])

    pool_p = u[:NP].reshape(BATCH, SEQ, POOL_DIM)[:, -POOL_STATE:]
    pool_s = u[NP:].reshape(DEC_BATCH, DEC_SEQ, POOL_DIM)[:, -POOL_STATE:]
    ckv_p = ckv[:NP].reshape(BATCH, SEQ, KV_LORA)
    ckv_s = ckv[NP:].reshape(DEC_BATCH, DEC_SEQ, KV_LORA)
    kpe_p = kpe128[:NP, :ROPE_DIM].reshape(BATCH, SEQ, ROPE_DIM)
    kpe_s = kpe128[NP:, :ROPE_DIM].reshape(DEC_BATCH, DEC_SEQ, ROPE_DIM)

    x = _ffn(x, ffn_pre_up[1], ffn_pre_down[1], ln_g[1, 0], ln_b[1, 0])
    w_ssd_x = jnp.pad(w_in_ssd.astype(BF16), ((0, 0), (0, SSD_IN_PAD - w_in_ssd.shape[1])))
    zx = _ssd_in(x, w_ssd_x)
    conv_w8 = jnp.pad(conv_w, ((0, SUBLANES - CONV_WIDTH), (0, 0)))
    pad_lanes = lambda v, fill: jnp.pad(v.astype(F32), (0, LANES - v.shape[0]), constant_values=fill).reshape(1, LANES)
    a_neg = -jnp.exp(pad_lanes(a_log, 0.0))
    d_skip_x = jnp.repeat(d_skip.astype(F32), SSM_HEAD_DIM).reshape(1, SSM_INNER)
    onehot = (jnp.arange(LANES)[:, None] == (jnp.arange(SSM_INNER) // SSM_HEAD_DIM)[None, :]).astype(BF16)
    common = (conv_w8, conv_b.reshape(1, -1), pad_lanes(dt_bias, 0.0), a_neg, d_skip_x,
              ssm_norm_g.reshape(1, -1), onehot)
    hist = SUBLANES - (CONV_WIDTH - 1)
    y_p, h_p = _ssd_core(zx, 0, NP // SSD_L, SSD_L, False,
                         jnp.zeros((BATCH, SUBLANES, CONV_DIM), F32),
                         jnp.zeros((BATCH, SSM_STATE, SSM_INNER), F32), *common)
    y_s, h_s = _ssd_core(zx, NP, DEC_BATCH, DEC_SEQ, True,
                         jnp.pad(state_conv, ((0, 0), (hist, 0), (0, 0))),
                         _state_to_t(state_ssm), *common)
    x = _ssd_out(x, jnp.concatenate([y_p, y_s], axis=0), w_out_ssd, ln_g[1, 1], ln_b[1, 1])
    x = _ffn(x, ffn_post_up[1], ffn_post_down[1], ln_g[1, 2], ln_b[1, 2])

    xbc = zx[:, SSM_INNER:SSM_INNER + CONV_DIM]
    conv_p = xbc[:NP].reshape(BATCH, SEQ, CONV_DIM)[:, -(CONV_WIDTH - 1):]
    conv_s = xbc[NP:].reshape(DEC_BATCH, DEC_SEQ, CONV_DIM)[:, -(CONV_WIDTH - 1):]
    ssm_p = _state_from_t(h_p)
    ssm_s = _state_from_t(h_s)

    y_prompt = x[:NP].reshape(BATCH, SEQ, D_MODEL)
    y_sample = x[NP:].reshape(DEC_BATCH, DEC_SEQ, D_MODEL)
    return (y_prompt, y_sample, pool_p, pool_s, ckv_p, ckv_s, kpe_p, kpe_s, conv_p, conv_s, ssm_p, ssm_s)
```

```python
import functools

import jax
import jax.numpy as jnp
from jax import lax
from jax.experimental import pallas as pl
from jax.experimental.pallas import tpu as pltpu

F32 = jnp.float32
BF16 = jnp.bfloat16

D_MODEL = 2048
BATCH = 1
SEQ = 16384
DEPTH = 2
DEC_BATCH = 16
DEC_SEQ = 64
PAST_LEN = 1024
CHUNK = 64
ALPHA = (2 * DEPTH) ** 0.25
LN_EPS = 1e-5
RMS_EPS = 1e-6
FF_DIM = 5504
POOL_WINDOWS = (2, 4, 8, 16)
POOL_DIM = 512
POOL_GROUP = 128
POOL_STATE = 15
NOPE_DIM = 128
ROPE_DIM = 64
V_DIM = 128
MLA_HEADS = 12
Q_LORA = 512
KV_LORA = 512
ROPE_THETA = 10000.0
SSM_INNER = 4096
SSM_HEAD_DIM = 64
SSM_HEADS = 64
SSM_GROUPS = 8
SSM_HPG = 8
SSM_STATE = 128
CONV_WIDTH = 4
CONV_DIM = SSM_INNER + 2 * SSM_GROUPS * SSM_STATE

NP = BATCH * SEQ
NS = DEC_BATCH * DEC_SEQ
NTOK = NP + NS

LANES = 128
SUBLANES = 8
VMEM_LIMIT_BYTES = 56 * 1024 * 1024

FF_PAD = 5632
FF_TILE = 512
FFN_TM = 512
TOK_TM = 512
ATT_TQ = 512
ATT_TK = 512
SSD_L = 128
QK_PAD = 256
SSD_IN_TM = 1024
SSD_IN_TN = 1024
POOL_EXT = 16
ATT_SCALE = float((NOPE_DIM + ROPE_DIM) ** -0.5)
ATT_SCALE_LOG2E = ATT_SCALE * 1.4426950408889634


def _cparams(sem):
    return pltpu.CompilerParams(dimension_semantics=sem, vmem_limit_bytes=VMEM_LIMIT_BYTES)


def _layer_norm(y, g, b):
    mu = jnp.mean(y, axis=-1, keepdims=True)
    d = y - mu
    var = jnp.mean(d * d, axis=-1, keepdims=True)
    return d * lax.rsqrt(var + LN_EPS) * g + b


def _rms(h, g):
    return h * lax.rsqrt(jnp.mean(h * h, axis=-1, keepdims=True) + RMS_EPS) * g


def _silu(x):
    return x * jax.nn.sigmoid(x)


def _dot(a, b):
    return jnp.dot(a, b, preferred_element_type=F32)


def _dot_nt(a, b):
    return lax.dot_general(a, b, (((1,), (1,)), ((), ())), preferred_element_type=F32)


def _dot_tn(a, b):
    return lax.dot_general(a, b, (((0,), (0,)), ((), ())), preferred_element_type=F32)


def _split2(v):
    hi = v.astype(BF16)
    lo = (v - hi.astype(F32)).astype(BF16)
    return hi, lo


def _split3(v):
    hi = v.astype(BF16)
    r = v - hi.astype(F32)
    mid = r.astype(BF16)
    lo = (r - mid.astype(F32)).astype(BF16)
    return hi, mid, lo


def _ffn_kernel(x_ref, wg_ref, wu_ref, wd_ref, g_ref, b_ref, o_ref, xb_ref):
    j = pl.program_id(1)

    @pl.when(j == 0)
    def _():
        xb_ref[...] = x_ref[...].astype(BF16)
        o_ref[...] = jnp.zeros_like(o_ref)

    xb = xb_ref[...]
    h = (_silu(_dot(xb, wg_ref[...])) * _dot(xb, wu_ref[...])).astype(BF16)
    o_ref[...] += _dot(h, wd_ref[...])

    @pl.when(j == pl.num_programs(1) - 1)
    def _():
        o_ref[...] = _layer_norm(ALPHA * x_ref[...] + 0.5 * o_ref[...], g_ref[...], b_ref[...])


def _ffn(x, w_up, w_down, g, b):
    pad = FF_PAD - FF_DIM
    wg = jnp.pad(w_up[:, :FF_DIM].astype(BF16), ((0, 0), (0, pad)))
    wu = jnp.pad(w_up[:, FF_DIM:].astype(BF16), ((0, 0), (0, pad)))
    wd = jnp.pad(w_down.astype(BF16), ((0, pad), (0, 0)))
    n = x.shape[0]
    return pl.pallas_call(
        _ffn_kernel,
        grid=(n // FFN_TM, FF_PAD // FF_TILE),
        in_specs=[
            pl.BlockSpec((FFN_TM, D_MODEL), lambda i, j: (i, 0)),
            pl.BlockSpec((D_MODEL, FF_TILE), lambda i, j: (0, j)),
            pl.BlockSpec((D_MODEL, FF_TILE), lambda i, j: (0, j)),
            pl.BlockSpec((FF_TILE, D_MODEL), lambda i, j: (j, 0)),
            pl.BlockSpec((1, D_MODEL), lambda i, j: (0, 0)),
            pl.BlockSpec((1, D_MODEL), lambda i, j: (0, 0)),
        ],
        out_specs=pl.BlockSpec((FFN_TM, D_MODEL), lambda i, j: (i, 0)),
        out_shape=jax.ShapeDtypeStruct((n, D_MODEL), F32),
        scratch_shapes=[pltpu.VMEM((FFN_TM, D_MODEL), BF16)],
        compiler_params=_cparams(("parallel", "arbitrary")),
        name="ffn_postnorm",
    )(x, wg, wu, wd, g.reshape(1, -1), b.reshape(1, -1))


def _pool_window_mean_minus(ext_ref, base, rows, pos0):
    first = base + POOL_EXT
    t = lax.broadcasted_iota(jnp.int32, (rows, POOL_GROUP), 0)
    posp1 = (pos0 + t + 1).astype(F32)
    outs = []
    for g, w in enumerate(POOL_WINDOWS):
        cols = slice(g * POOL_GROUP, (g + 1) * POOL_GROUP)
        cur = ext_ref[pl.ds(first, rows), cols]
        tot = cur
        for s in range(1, w):
            tot = tot + ext_ref[pl.ds(first - s, rows), cols]
        outs.append(tot / jnp.minimum(posp1, float(w)) - cur)
    return outs


def _ab_in_kernel(n_prompt_tiles, x_ref, w_ref, past_ref, pw_ref, ps_ref, qg_ref, kg_ref, rope_ref,
                  u_ref, yp_ref, qn_ref, ckv_ref, kpe_ref, ext_ref):
    i = pl.program_id(0)
    tm = x_ref.shape[0]
    h = _dot(x_ref[...].astype(BF16), w_ref[...])
    u = h[:, :POOL_DIM]
    u_ref[...] = u
    qn_ref[...] = _rms(h[:, POOL_DIM:POOL_DIM + Q_LORA], qg_ref[...]).astype(BF16)
    ckv_ref[...] = _rms(h[:, POOL_DIM + Q_LORA:POOL_DIM + Q_LORA + KV_LORA], kg_ref[...])
    kv = h[:, POOL_DIM + Q_LORA + KV_LORA:] * rope_ref[...]
    lane = lax.broadcasted_iota(jnp.int32, kv.shape, 1)
    kpe_ref[...] = jnp.where(lane < ROPE_DIM, kv + pltpu.roll(kv, ROPE_DIM, axis=1), 0.0)

    def finish(d_groups):
        for g in range(len(POOL_WINDOWS)):
            cols = slice(g * POOL_GROUP, (g + 1) * POOL_GROUP)
            y = _dot(d_groups[g].astype(BF16), pw_ref[g]) * ps_ref[:, cols]
            yp_ref[:, cols] = y.astype(BF16)

    @pl.when(i < n_prompt_tiles)
    def _():
        @pl.when(i == 0)
        def _():
            ext_ref[pl.ds(0, POOL_EXT), :] = jnp.zeros((POOL_EXT, POOL_DIM), F32)

        ext_ref[pl.ds(POOL_EXT, tm), :] = u
        finish(_pool_window_mean_minus(ext_ref, 0, tm, i * tm))
        ext_ref[pl.ds(0, POOL_EXT), :] = u[tm - POOL_EXT:, :]

    @pl.when(i >= n_prompt_tiles)
    def _():
        nseg = tm // DEC_SEQ
        stride = POOL_EXT + DEC_SEQ
        parts = [[] for _ in POOL_WINDOWS]
        for s in range(nseg):
            ext_ref[pl.ds(s * stride, POOL_EXT), :] = past_ref[s]
            ext_ref[pl.ds(s * stride + POOL_EXT, DEC_SEQ), :] = u[s * DEC_SEQ:(s + 1) * DEC_SEQ, :]
        for s in range(nseg):
            d = _pool_window_mean_minus(ext_ref, s * stride, DEC_SEQ, PAST_LEN)
            for g in range(len(POOL_WINDOWS)):
                parts[g].append(d[g])
        finish([jnp.concatenate(p, axis=0) for p in parts])


def _ab_in(x, w_in_x, pool_past, pool_w, pool_scale, q_norm_g, kv_norm_g, rope_k):
    n = x.shape[0]
    tm = TOK_TM
    nseg = tm // DEC_SEQ
    npt = NP // tm
    wcols = w_in_x.shape[1]
    ext_rows = max(tm + POOL_EXT, nseg * (POOL_EXT + DEC_SEQ))
    const = lambda i: (0, 0)
    return pl.pallas_call(
        functools.partial(_ab_in_kernel, npt),
        grid=(n // tm,),
        in_specs=[
            pl.BlockSpec((tm, D_MODEL), lambda i: (i, 0)),
            pl.BlockSpec((D_MODEL, wcols), const),
            pl.BlockSpec((nseg, POOL_EXT, POOL_DIM), lambda i: (jnp.maximum(i - npt, 0), 0, 0)),
            pl.BlockSpec((len(POOL_WINDOWS), POOL_GROUP, POOL_GROUP), lambda i: (0, 0, 0)),
            pl.BlockSpec((1, POOL_DIM), const),
            pl.BlockSpec((1, Q_LORA), const),
            pl.BlockSpec((1, KV_LORA), const),
            pl.BlockSpec((tm, LANES), lambda i: (i, 0)),
        ],
        out_specs=[
            pl.BlockSpec((tm, POOL_DIM), lambda i: (i, 0)),
            pl.BlockSpec((tm, POOL_DIM), lambda i: (i, 0)),
            pl.BlockSpec((tm, Q_LORA), lambda i: (i, 0)),
            pl.BlockSpec((tm, KV_LORA), lambda i: (i, 0)),
            pl.BlockSpec((tm, LANES), lambda i: (i, 0)),
        ],
        out_shape=[
            jax.ShapeDtypeStruct((n, POOL_DIM), F32),
            jax.ShapeDtypeStruct((n, POOL_DIM), BF16),
            jax.ShapeDtypeStruct((n, Q_LORA), BF16),
            jax.ShapeDtypeStruct((n, KV_LORA), F32),
            jax.ShapeDtypeStruct((n, LANES), F32),
        ],
        scratch_shapes=[pltpu.VMEM((ext_rows, POOL_DIM), F32)],
        compiler_params=_cparams(("arbitrary",)),
        name="ab_in",
    )(x, w_in_x, pool_past, pool_w.astype(BF16), pool_scale.reshape(1, -1),
      q_norm_g.reshape(1, -1), kv_norm_g.reshape(1, -1), rope_k)


def _q_up_kernel(qn_ref, w_ref, rope_ref, q_ref):
    qn = qn_ref[...]
    rope = rope_ref[...]
    lane = lax.broadcasted_iota(jnp.int32, rope.shape, 1)
    hpc = 4
    ncol = hpc * LANES
    for c in range(MLA_HEADS // hpc):
        nope = _dot(qn, w_ref[:, c * ncol:(c + 1) * ncol])
        rp = _dot(qn, w_ref[:, MLA_HEADS * LANES + c * ncol:MLA_HEADS * LANES + (c + 1) * ncol])
        for hh in range(hpc):
            hd = c * hpc + hh
            v = rp[:, hh * LANES:(hh + 1) * LANES] * rope
            rot = jnp.where(lane < ROPE_DIM, v + pltpu.roll(v, ROPE_DIM, axis=1), 0.0)
            q_ref[hd, :, 0:LANES] = (nope[:, hh * LANES:(hh + 1) * LANES] * ATT_SCALE).astype(BF16)
            q_ref[hd, :, LANES:QK_PAD] = (rot * ATT_SCALE).astype(BF16)


def _q_up(qn, wq_x, rope_q, row0, nrows):
    tm = TOK_TM
    t0 = row0 // tm
    return pl.pallas_call(
        _q_up_kernel,
        grid=(nrows // tm,),
        in_specs=[
            pl.BlockSpec((tm, Q_LORA), lambda i: (t0 + i, 0)),
            pl.BlockSpec(wq_x.shape, lambda i: (0, 0)),
            pl.BlockSpec((tm, LANES), lambda i: (t0 + i, 0)),
        ],
        out_specs=pl.BlockSpec((MLA_HEADS, tm, QK_PAD), lambda i: (0, i, 0)),
        out_shape=jax.ShapeDtypeStruct((MLA_HEADS, nrows, QK_PAD), BF16),
        compiler_params=_cparams(("parallel",)),
        name="q_up",
    )(qn, wq_x, rope_q)


def _q_up_t_kernel(qn_ref, wt_ref, ropet_ref, qt_ref):
    qt_all = _dot_nt(wt_ref[...], qn_ref[...])
    ropet = ropet_ref[...]
    nrope = MLA_HEADS * LANES
    zeros = jnp.zeros((QK_PAD - LANES - ROPE_DIM, qt_all.shape[1]), BF16)
    for hd in range(MLA_HEADS):
        v = qt_all[nrope + hd * LANES:nrope + (hd + 1) * LANES, :] * ropet
        rot = v[:ROPE_DIM] + v[ROPE_DIM:]
        qt_ref[hd, 0, 0:LANES, :] = (qt_all[hd * LANES:(hd + 1) * LANES, :] * ATT_SCALE_LOG2E).astype(BF16)
        qt_ref[hd, 0, LANES:LANES + ROPE_DIM, :] = (rot * ATT_SCALE_LOG2E).astype(BF16)
        qt_ref[hd, 0, LANES + ROPE_DIM:QK_PAD, :] = zeros


def _q_up_t(qn, wq_xt, rope_t, nrows):
    tm = ATT_TQ
    return pl.pallas_call(
        _q_up_t_kernel,
        grid=(nrows // tm,),
        in_specs=[
            pl.BlockSpec((tm, Q_LORA), lambda i: (i, 0)),
            pl.BlockSpec(wq_xt.shape, lambda i: (0, 0)),
            pl.BlockSpec((LANES, tm), lambda i: (0, i)),
        ],
        out_specs=pl.BlockSpec((MLA_HEADS, 1, QK_PAD, tm), lambda i: (0, i, 0, 0)),
        out_shape=jax.ShapeDtypeStruct((MLA_HEADS, nrows // tm, QK_PAD, tm), BF16),
        compiler_params=_cparams(("parallel",)),
        name="q_up_t",
    )(qn, wq_xt, rope_t)


def _kv_up_kernel(ckv_ref, kpe_ref, w_ref, k_ref, v_ref):
    c = ckv_ref[...].astype(BF16)
    kpe = kpe_ref[...].astype(BF16)
    hpc = 4
    ncol = hpc * LANES
    for j in range(MLA_HEADS // hpc):
        kn = _dot(c, w_ref[:, j * ncol:(j + 1) * ncol])
        vv = _dot(c, w_ref[:, MLA_HEADS * LANES + j * ncol:MLA_HEADS * LANES + (j + 1) * ncol])
        for hh in range(hpc):
            hd = j * hpc + hh
            k_ref[hd, :, 0:LANES] = kn[:, hh * LANES:(hh + 1) * LANES].astype(BF16)
            k_ref[hd, :, LANES:QK_PAD] = kpe
            v_ref[hd] = vv[:, hh * LANES:(hh + 1) * LANES].astype(BF16)


def _kv_up(ckv, kpe128, wkv_x, row0, nrows):
    tm = TOK_TM
    t0 = row0 // tm
    return pl.pallas_call(
        _kv_up_kernel,
        grid=(nrows // tm,),
        in_specs=[
            pl.BlockSpec((tm, KV_LORA), lambda i: (t0 + i, 0)),
            pl.BlockSpec((tm, LANES), lambda i: (t0 + i, 0)),
            pl.BlockSpec(wkv_x.shape, lambda i: (0, 0)),
        ],
        out_specs=[
            pl.BlockSpec((MLA_HEADS, tm, QK_PAD), lambda i: (0, i, 0)),
            pl.BlockSpec((MLA_HEADS, tm, V_DIM), lambda i: (0, i, 0)),
        ],
        out_shape=[
            jax.ShapeDtypeStruct((MLA_HEADS, nrows, QK_PAD), BF16),
            jax.ShapeDtypeStruct((MLA_HEADS, nrows, V_DIM), BF16),
        ],
        compiler_params=_cparams(("parallel",)),
        name="kv_up",
    )(ckv, kpe128, wkv_x)


def _kv_up_t_kernel(ckv_ref, kpe_ref, wk_ref, wvt_ref, k_ref, vt_ref):
    c = ckv_ref[...].astype(BF16)
    kpe = kpe_ref[...].astype(BF16)
    hpc = 4
    ncol = hpc * LANES
    for j in range(MLA_HEADS // hpc):
        kn = _dot(c, wk_ref[:, j * ncol:(j + 1) * ncol])
        for hh in range(hpc):
            hd = j * hpc + hh
            k_ref[hd, :, 0:LANES] = kn[:, hh * LANES:(hh + 1) * LANES].astype(BF16)
            k_ref[hd, :, LANES:QK_PAD] = kpe
    vt_all = _dot_nt(wvt_ref[...], c)
    for hd in range(MLA_HEADS):
        vt_ref[hd, 0] = vt_all[hd * V_DIM:(hd + 1) * V_DIM, :].astype(BF16)


def _kv_up_t(ckv, kpe128, wk, wvt, nrows):
    tm = ATT_TK
    return pl.pallas_call(
        _kv_up_t_kernel,
        grid=(nrows // tm,),
        in_specs=[
            pl.BlockSpec((tm, KV_LORA), lambda i: (i, 0)),
            pl.BlockSpec((tm, LANES), lambda i: (i, 0)),
            pl.BlockSpec(wk.shape, lambda i: (0, 0)),
            pl.BlockSpec(wvt.shape, lambda i: (0, 0)),
        ],
        out_specs=[
            pl.BlockSpec((MLA_HEADS, tm, QK_PAD), lambda i: (0, i, 0)),
            pl.BlockSpec((MLA_HEADS, 1, V_DIM, tm), lambda i: (0, i, 0, 0)),
        ],
        out_shape=[
            jax.ShapeDtypeStruct((MLA_HEADS, nrows, QK_PAD), BF16),
            jax.ShapeDtypeStruct((MLA_HEADS, nrows // tm, V_DIM, tm), BF16),
        ],
        compiler_params=_cparams(("parallel",)),
        name="kv_up_t",
    )(ckv, kpe128, wk, wvt)


def _attn_prompt_kernel(qt_ref, k_ref, vt_ref, o_ref, s0_ref, s1_ref, m_ref, l_ref, acc_ref):
    i = pl.program_id(1)
    tk, tq = s0_ref.shape

    def scores(c, s_ref):
        start = pl.multiple_of(c * tk, tk)
        s_ref[...] = _dot(k_ref[0, pl.ds(start, tk), :], qt_ref[0, 0])

    def softmax_pv(c, s_ref, mask):
        st = s_ref[...]
        if mask is not None:
            st = jnp.where(mask, st, -jnp.inf)
        m = m_ref[...]
        m_new = jnp.maximum(m, jnp.max(st, axis=0, keepdims=True))
        alpha = jnp.exp2(m - m_new)
        p = jnp.exp2(st - m_new)
        l_ref[...] = alpha * l_ref[...] + jnp.sum(p, axis=0, keepdims=True)
        m_ref[...] = m_new
        acc_ref[...] = alpha * acc_ref[...] + _dot(vt_ref[0, c], p.astype(BF16))

    m_ref[...] = jnp.full(m_ref.shape, -jnp.inf, F32)
    l_ref[...] = jnp.zeros(l_ref.shape, F32)
    acc_ref[...] = jnp.zeros(acc_ref.shape, F32)
    key_chunk = lax.broadcasted_iota(jnp.int32, (tk, tq), 0) // CHUNK
    query_chunk = lax.broadcasted_iota(jnp.int32, (tk, tq), 1) // CHUNK
    diag = query_chunk >= key_chunk

    scores(0, s0_ref)

    def pair(j, carry):
        scores(2 * j + 1, s1_ref)
        softmax_pv(2 * j, s0_ref, None)
        scores(2 * j + 2, s0_ref)
        softmax_pv(2 * j + 1, s1_ref, None)
        return carry

    lax.fori_loop(0, i // 2, pair, 0)

    @pl.when(i % 2 == 0)
    def _():
        softmax_pv(i, s0_ref, diag)

    @pl.when(i % 2 == 1)
    def _():
        scores(i, s1_ref)
        softmax_pv(i - 1, s0_ref, None)
        softmax_pv(i, s1_ref, diag)

    o_ref[...] = (acc_ref[...] / l_ref[...]).T.astype(BF16)


def _attn_prompt(qt, k, vt):
    tq, tk = ATT_TQ, ATT_TK
    assert tq == tk and tq % CHUNK == 0
    return pl.pallas_call(
        _attn_prompt_kernel,
        grid=(MLA_HEADS, NP // tq),
        in_specs=[
            pl.BlockSpec((1, 1, QK_PAD, tq), lambda h, i: (h, i, 0, 0)),
            pl.BlockSpec((1, NP, QK_PAD), lambda h, i: (h, 0, 0)),
            pl.BlockSpec((1, NP // tk, V_DIM, tk), lambda h, i: (h, 0, 0, 0)),
        ],
        out_specs=pl.BlockSpec((tq, V_DIM), lambda h, i: (i, h)),
        out_shape=jax.ShapeDtypeStruct((NP, MLA_HEADS * V_DIM), BF16),
        scratch_shapes=[
            pltpu.VMEM((tk, tq), F32), pltpu.VMEM((tk, tq), F32),
            pltpu.VMEM((1, tq), F32), pltpu.VMEM((1, tq), F32), pltpu.VMEM((V_DIM, tq), F32),
        ],
        compiler_params=_cparams(("parallel", "arbitrary")),
        name="attn_prompt",
    )(qt, k, vt)


def _attn_sample_kernel(q_ref, kp_ref, vp_ref, kn_ref, vn_ref, o_ref):
    for hd in range(MLA_HEADS):
        q = q_ref[hd]
        sp = _dot_nt(q, kp_ref[hd])
        sn = _dot_nt(q, kn_ref[hd])
        m = jnp.maximum(jnp.max(sp, axis=1, keepdims=True), jnp.max(sn, axis=1, keepdims=True))
        pp = jnp.exp(sp - m)
        pn = jnp.exp(sn - m)
        l = jnp.sum(pp, axis=1, keepdims=True) + jnp.sum(pn, axis=1, keepdims=True)
        o = _dot(pp.astype(BF16), vp_ref[hd]) + _dot(pn.astype(BF16), vn_ref[hd])
        o_ref[:, hd * V_DIM:(hd + 1) * V_DIM] = (o / l).astype(BF16)


def _attn_sample(q, k_past, v_past, k_new, v_new):
    assert PAST_LEN % CHUNK == 0 and DEC_SEQ <= CHUNK
    return pl.pallas_call(
        _attn_sample_kernel,
        grid=(DEC_BATCH,),
        in_specs=[
            pl.BlockSpec((MLA_HEADS, DEC_SEQ, QK_PAD), lambda b: (0, b, 0)),
            pl.BlockSpec((MLA_HEADS, PAST_LEN, QK_PAD), lambda b: (0, b, 0)),
            pl.BlockSpec((MLA_HEADS, PAST_LEN, V_DIM), lambda b: (0, b, 0)),
            pl.BlockSpec((MLA_HEADS, DEC_SEQ, QK_PAD), lambda b: (0, b, 0)),
            pl.BlockSpec((MLA_HEADS, DEC_SEQ, V_DIM), lambda b: (0, b, 0)),
        ],
        out_specs=pl.BlockSpec((DEC_SEQ, MLA_HEADS * V_DIM), lambda b: (b, 0)),
        out_shape=jax.ShapeDtypeStruct((NS, MLA_HEADS * V_DIM), BF16),
        compiler_params=_cparams(("parallel",)),
        name="attn_sample",
    )(q, k_past, v_past, k_new, v_new)


def _ab_out_kernel(npt, x_ref, yp_ref, ap_ref, as_ref, wp_ref, wa_ref, g_ref, b_ref, o_ref):
    i = pl.program_id(0)
    att = jnp.where(i < npt, ap_ref[...], as_ref[...])
    mix = _dot(yp_ref[...], wp_ref[...]) + _dot(att, wa_ref[...])
    o_ref[...] = _layer_norm(ALPHA * x_ref[...] + mix, g_ref[...], b_ref[...])


def _ab_out(x, ypool, att_p, att_s, w_out, g, b):
    n = x.shape[0]
    tm = TOK_TM
    npt = NP // tm
    w = w_out.astype(BF16)
    adim = MLA_HEADS * V_DIM
    const = lambda i: (0, 0)
    return pl.pallas_call(
        functools.partial(_ab_out_kernel, npt),
        grid=(n // tm,),
        in_specs=[
            pl.BlockSpec((tm, D_MODEL), lambda i: (i, 0)),
            pl.BlockSpec((tm, POOL_DIM), lambda i: (i, 0)),
            pl.BlockSpec((tm, adim), lambda i: (jnp.minimum(i, npt - 1), 0)),
            pl.BlockSpec((tm, adim), lambda i: (jnp.maximum(i - npt, 0), 0)),
            pl.BlockSpec((POOL_DIM, D_MODEL), const),
            pl.BlockSpec((adim, D_MODEL), const),
            pl.BlockSpec((1, D_MODEL), const),
            pl.BlockSpec((1, D_MODEL), const),
        ],
        out_specs=pl.BlockSpec((tm, D_MODEL), lambda i: (i, 0)),
        out_shape=jax.ShapeDtypeStruct((n, D_MODEL), F32),
        compiler_params=_cparams(("parallel",)),
        name="ab_out",
    )(x, ypool, att_p, att_s, w[:POOL_DIM], w[POOL_DIM:], g.reshape(1, -1), b.reshape(1, -1))


def _ssd_out_kernel(npt, x_ref, ap_ref, as_ref, w_ref, g_ref, b_ref, o_ref):
    act = jnp.where(pl.program_id(0) < npt, ap_ref[...], as_ref[...])
    o_ref[...] = _layer_norm(ALPHA * x_ref[...] + _dot(act, w_ref[...]), g_ref[...], b_ref[...])


def _ssd_out(x, act_p, act_s, w_out, g, b):
    n = x.shape[0]
    tm = TOK_TM
    npt = NP // tm
    const = lambda i: (0, 0)
    return pl.pallas_call(
        functools.partial(_ssd_out_kernel, npt),
        grid=(n // tm,),
        in_specs=[
            pl.BlockSpec((tm, D_MODEL), lambda i: (i, 0)),
            pl.BlockSpec((tm, SSM_INNER), lambda i: (jnp.minimum(i, npt - 1), 0)),
            pl.BlockSpec((tm, SSM_INNER), lambda i: (jnp.maximum(i - npt, 0), 0)),
            pl.BlockSpec((SSM_INNER, D_MODEL), const),
            pl.BlockSpec((1, D_MODEL), const),
            pl.BlockSpec((1, D_MODEL), const),
        ],
        out_specs=pl.BlockSpec((tm, D_MODEL), lambda i: (i, 0)),
        out_shape=jax.ShapeDtypeStruct((n, D_MODEL), F32),
        compiler_params=_cparams(("parallel",)),
        name="ssd_out",
    )(x, act_p, act_s, w_out.astype(BF16), g.reshape(1, -1), b.reshape(1, -1))


def _ssd_in_kernel(x_ref, w_ref, wdt_ref, o_ref, dt_ref, xb_ref):
    @pl.when(pl.program_id(1) == 0)
    def _():
        xb_ref[...] = x_ref[...].astype(BF16)
        dt_ref[...] = _dot(xb_ref[...], wdt_ref[...])

    o_ref[...] = _dot(xb_ref[...], w_ref[...])


def _ssd_in(x, w_zx, w_dt):
    n = x.shape[0]
    tm = SSD_IN_TM
    ncol = w_zx.shape[1]
    return pl.pallas_call(
        _ssd_in_kernel,
        grid=(n // tm, ncol // SSD_IN_TN),
        in_specs=[
            pl.BlockSpec((tm, D_MODEL), lambda i, j: (i, 0)),
            pl.BlockSpec((D_MODEL, SSD_IN_TN), lambda i, j: (0, j)),
            pl.BlockSpec((D_MODEL, LANES), lambda i, j: (0, 0)),
        ],
        out_specs=[
            pl.BlockSpec((tm, SSD_IN_TN), lambda i, j: (i, j)),
            pl.BlockSpec((tm, LANES), lambda i, j: (i, 0)),
        ],
        out_shape=[
            jax.ShapeDtypeStruct((n, ncol), F32),
            jax.ShapeDtypeStruct((n, LANES), F32),
        ],
        scratch_shapes=[pltpu.VMEM((tm, D_MODEL), BF16)],
        compiler_params=_cparams(("parallel", "arbitrary")),
        name="ssd_in",
    )(x, w_zx, w_dt)


def _ssd_kernel(rows, fresh_each_step, z_ref, xs_ref, bc_ref, dt_ref, cpast_ref, h0_ref, cw_ref,
                cb_ref, dtb_ref, a_ref, dsk_ref, ng_ref, e_ref, y_ref, h_ref, ext_ref):
    c = pl.program_id(0)
    L = SSD_L
    G, R, P, S = SSM_GROUPS, SSM_HPG, SSM_HEAD_DIM, SSM_STATE
    halo = SUBLANES

    def start_sequence():
        ext_ref[pl.ds(0, halo), :] = cpast_ref[0]
        h_ref[0] = h0_ref[0]

    if fresh_each_step:
        start_sequence()
    else:
        pl.when(c == 0)(start_sequence)

    ext_ref[pl.ds(halo, rows), 0:SSM_INNER] = xs_ref[...]
    ext_ref[pl.ds(halo, rows), SSM_INNER:CONV_DIM] = bc_ref[...]
    if rows < L:
        ext_ref[pl.ds(halo + rows, L - rows), :] = jnp.zeros((L - rows, CONV_DIM), F32)

    conv = cb_ref[...] + cw_ref[0:1, :] * ext_ref[pl.ds(halo - 3, L), :]
    for k in range(1, CONV_WIDTH):
        conv = conv + cw_ref[k:k + 1, :] * ext_ref[pl.ds(halo - 3 + k, L), :]
    ext_ref[pl.ds(0, halo), :] = ext_ref[pl.ds(rows, halo), :]
    act = _silu(conv)
    xs = act[:, :SSM_INNER]
    xsb = xs.astype(BF16)
    bmat = act[:, SSM_INNER:SSM_INNER + G * S].astype(BF16)
    cmat = act[:, SSM_INNER + G * S:].astype(BF16)

    dtr = dt_ref[...] + dtb_ref[...]
    dt = jnp.maximum(dtr, 0.0) + jnp.log1p(jnp.exp(-jnp.abs(dtr)))
    if rows < L:
        dt = jnp.concatenate([dt, jnp.zeros((L - rows, LANES), F32)], axis=0)
    da = dt * a_ref[...]
    ti = lax.broadcasted_iota(jnp.int32, (L, L), 0)
    si = lax.broadcasted_iota(jnp.int32, (L, L), 1)
    causal = ti >= si
    tri = jnp.where(causal, 1.0, 0.0).astype(BF16)
    acs = sum(_dot(tri, part) for part in _split3(da))
    acs_t = acs.T
    dt_t = dt.T
    last = acs[L - 1:L, :]
    to_end = jnp.exp(last - acs) * dt
    eacs = jnp.exp(acs)
    onehot = e_ref[...]
    expand = lambda v: sum(_dot(part, onehot) for part in _split2(v))
    x_end = (xs * expand(to_end)).astype(BF16)
    eacs_x = expand(eacs)
    dec_x = sum(_dot(part, onehot) for part in _split3(jnp.broadcast_to(jnp.exp(last), (SUBLANES, LANES))))[0:1, :]

    lane4 = lax.broadcasted_iota(jnp.int32, (L, 4 * P), 1) // P
    for g in range(G):
        bg = bmat[:, g * S:(g + 1) * S]
        cg = cmat[:, g * S:(g + 1) * S]
        cbm = _dot_nt(cg, bg)
        cols = slice(g * R * P, (g + 1) * R * P)
        ht = h_ref[0, :, cols]
        y_state = _dot(cg, ht.astype(BF16)) * eacs_x[:, cols]
        y_parts = []
        for half in range(R // 4):
            x4 = xsb[:, g * R * P + half * 4 * P:g * R * P + (half + 1) * 4 * P]
            x_bd = jnp.concatenate([jnp.where(lane4 == r, x4, jnp.zeros_like(x4)) for r in range(4)], axis=0)
            wts = []
            for r in range(4):
                hd = g * R + half * 4 + r
                seg = acs[:, hd:hd + 1] - acs_t[hd:hd + 1, :]
                decay = jnp.exp(jnp.where(causal, seg, -jnp.inf))
                wts.append((cbm * decay * dt_t[hd:hd + 1, :]).astype(BF16))
            y_parts.append(_dot(jnp.concatenate(wts, axis=1), x_bd))
        y = jnp.concatenate(y_parts, axis=1) + y_state + dsk_ref[:, cols] * xs[:, cols]
        gz = y[:rows] * _silu(z_ref[:, cols])
        gz = gz * lax.rsqrt(jnp.mean(gz * gz, axis=-1, keepdims=True) + RMS_EPS)
        y_ref[:, cols] = (gz * ng_ref[:, cols]).astype(BF16)
        h_ref[0, :, cols] = ht * dec_x[:, cols] + _dot_tn(bg, x_end[:, cols])


def _ssd_core(zx, dtx, row0, nblk, rows, fresh_each_step, conv_past8, h0_t, conv_w8, conv_b, dt_bias, a_neg,
              d_skip_x, norm_g, onehot):
    L = SSD_L
    r0 = row0 // rows
    nseq = nblk if fresh_each_step else 1
    seq = (lambda c: c) if fresh_each_step else (lambda c: 0)
    const = lambda c: (0, 0)
    return pl.pallas_call(
        functools.partial(_ssd_kernel, rows, fresh_each_step),
        grid=(nblk,),
        in_specs=[
            pl.BlockSpec((rows, SSM_INNER), lambda c: (r0 + c, 0)),
            pl.BlockSpec((rows, SSM_INNER), lambda c: (r0 + c, 1)),
            pl.BlockSpec((rows, 2 * SSM_GROUPS * SSM_STATE), lambda c: (r0 + c, 4)),
            pl.BlockSpec((rows, LANES), lambda c: (r0 + c, 0)),
            pl.BlockSpec((1, SUBLANES, CONV_DIM), lambda c: (seq(c), 0, 0)),
            pl.BlockSpec((1, SSM_STATE, SSM_INNER), lambda c: (seq(c), 0, 0)),
            pl.BlockSpec((SUBLANES, CONV_DIM), const),
            pl.BlockSpec((1, CONV_DIM), const),
            pl.BlockSpec((1, LANES), const),
            pl.BlockSpec((1, LANES), const),
            pl.BlockSpec((1, SSM_INNER), const),
            pl.BlockSpec((1, SSM_INNER), const),
            pl.BlockSpec((LANES, SSM_INNER), const),
        ],
        out_specs=[
            pl.BlockSpec((rows, SSM_INNER), lambda c: (c, 0)),
            pl.BlockSpec((1, SSM_STATE, SSM_INNER), lambda c: (seq(c), 0, 0)),
        ],
        out_shape=[
            jax.ShapeDtypeStruct((nblk * rows, SSM_INNER), BF16),
            jax.ShapeDtypeStruct((nseq, SSM_STATE, SSM_INNER), F32),
        ],
        scratch_shapes=[pltpu.VMEM((L + 2 * SUBLANES, CONV_DIM), F32)],
        compiler_params=_cparams(("arbitrary",)),
        name="ssd_core_seq" if not fresh_each_step else "ssd_core_blocks",
    )(zx, zx, zx, dtx, conv_past8, h0_t, conv_w8, conv_b, dt_bias, a_neg, d_skip_x, norm_g, onehot)


def _rope_table():
    half = ROPE_DIM // 2
    inv = 1.0 / (ROPE_THETA ** (jnp.arange(half, dtype=F32) * 2.0 / ROPE_DIM))
    pos = jnp.concatenate([
        jnp.tile(jnp.arange(SEQ, dtype=F32), BATCH),
        jnp.tile(jnp.arange(DEC_SEQ, dtype=F32) + PAST_LEN, DEC_BATCH)])
    ang = pos[:, None] * inv[None, :]
    cos, sin = jnp.cos(ang), jnp.sin(ang)
    return jnp.concatenate([cos, cos, -sin, sin], axis=1)


def _swap_halves(w):
    half = w.shape[-1] // 2
    return jnp.concatenate([w[..., half:], w[..., :half]], axis=-1)


def _prep_ab_weights(w_in_ab, w_q_up, w_kv_up):
    kpe_w = w_in_ab[:, POOL_DIM + Q_LORA + KV_LORA:]
    w_in_x = jnp.concatenate([w_in_ab, _swap_halves(kpe_w)], axis=1).astype(BF16)
    wq = w_q_up.reshape(Q_LORA, MLA_HEADS, NOPE_DIM + ROPE_DIM)
    wq_rope = wq[..., NOPE_DIM:]
    wq_x = jnp.concatenate([
        wq[..., :NOPE_DIM].reshape(Q_LORA, -1),
        jnp.concatenate([wq_rope, _swap_halves(wq_rope)], axis=-1).reshape(Q_LORA, -1)], axis=1).astype(BF16)
    wkv = w_kv_up.reshape(KV_LORA, MLA_HEADS, NOPE_DIM + V_DIM)
    wk = wkv[..., :NOPE_DIM].reshape(KV_LORA, -1).astype(BF16)
    wv = wkv[..., NOPE_DIM:].reshape(KV_LORA, -1).astype(BF16)
    return w_in_x, wq_x, wk, wv


def _state_to_t(h):
    b = h.shape[0]
    return jnp.transpose(h, (0, 3, 1, 2)).reshape(b, SSM_STATE, SSM_INNER)


def _state_from_t(ht):
    b = ht.shape[0]
    return jnp.transpose(ht.reshape(b, SSM_STATE, SSM_HEADS, SSM_HEAD_DIM), (0, 2, 3, 1))


def kernel(x_prompt, x_sample, state_pool, cache_ckv, cache_kpe, state_conv, state_ssm, ffn_pre_up, ffn_pre_down, ffn_post_up, ffn_post_down, ln_g, ln_b, w_in_ab, pool_w, pool_scale, q_norm_g, w_q_up, kv_norm_g, w_kv_up, w_out_ab, w_in_ssd, conv_w, conv_b, dt_bias, a_log, d_skip, ssm_norm_g, w_out_ssd):
    x = jnp.concatenate([x_prompt.reshape(NP, D_MODEL), x_sample.reshape(NS, D_MODEL)], axis=0)

    x = _ffn(x, ffn_pre_up[0], ffn_pre_down[0], ln_g[0, 0], ln_b[0, 0])
    rope = _rope_table()
    w_in_x, wq_x, wk, wv = _prep_ab_weights(w_in_ab, w_q_up, w_kv_up)
    wkv_x = jnp.concatenate([wk, wv], axis=1)
    pool_past = jnp.pad(state_pool, ((0, 0), (POOL_EXT - POOL_STATE, 0), (0, 0)))
    u, ypool, qn, ckv, kpe128 = _ab_in(x, w_in_x, pool_past, pool_w, pool_scale, q_norm_g, kv_norm_g, rope)
    qt_p = _q_up_t(qn, wq_x.T, rope.T, NP)
    k_p, vt_p = _kv_up_t(ckv, kpe128, wk, wv.T, NP)
    q_s = _q_up(qn, wq_x, rope, NP, NS)
    k_s, v_s = _kv_up(ckv, kpe128, wkv_x, NP, NS)
    k_past, v_past = _kv_up(cache_ckv.reshape(DEC_BATCH * PAST_LEN, KV_LORA),
                            jnp.pad(cache_kpe.reshape(DEC_BATCH * PAST_LEN, ROPE_DIM), ((0, 0), (0, LANES - ROPE_DIM))),
                            wkv_x, 0, DEC_BATCH * PAST_LEN)
    att_p = _attn_prompt(qt_p, k_p, vt_p)
    att_s = _attn_sample(q_s, k_past, v_past, k_s, v_s)
    x = _ab_out(x, ypool, att_p, att_s, w_out_ab, ln_g[0, 1], ln_b[0, 1])
    x = _ffn(x, ffn_post_up[0], ffn_post_down[0], ln_g[0, 2], ln_b[0, 2])

    pool_p = u[:NP].reshape(BATCH, SEQ, POOL_DIM)[:, -POOL_STATE:]
    pool_s = u[NP:].reshape(DEC_BATCH, DEC_SEQ, POOL_DIM)[:, -POOL_STATE:]
    ckv_p = ckv[:NP].reshape(BATCH, SEQ, KV_LORA)
    ckv_s = ckv[NP:].reshape(DEC_BATCH, DEC_SEQ, KV_LORA)
    kpe_p = kpe128[:NP, :ROPE_DIM].reshape(BATCH, SEQ, ROPE_DIM)
    kpe_s = kpe128[NP:, :ROPE_DIM].reshape(DEC_BATCH, DEC_SEQ, ROPE_DIM)

    x = _ffn(x, ffn_pre_up[1], ffn_pre_down[1], ln_g[1, 0], ln_b[1, 0])
    nzx = SSM_INNER + CONV_DIM
    w_ssd = w_in_ssd.astype(BF16)
    zx, dtx = _ssd_in(x, w_ssd[:, :nzx], jnp.pad(w_ssd[:, nzx:], ((0, 0), (0, LANES - SSM_HEADS))))
    conv_w8 = jnp.pad(conv_w, ((0, SUBLANES - CONV_WIDTH), (0, 0)))
    pad_lanes = lambda v, fill: jnp.pad(v.astype(F32), (0, LANES - v.shape[0]), constant_values=fill).reshape(1, LANES)
    a_neg = -jnp.exp(pad_lanes(a_log, 0.0))
    d_skip_x = jnp.repeat(d_skip.astype(F32), SSM_HEAD_DIM).reshape(1, SSM_INNER)
    onehot = (jnp.arange(LANES)[:, None] == (jnp.arange(SSM_INNER) // SSM_HEAD_DIM)[None, :]).astype(BF16)
    common = (conv_w8, conv_b.reshape(1, -1), pad_lanes(dt_bias, 0.0), a_neg, d_skip_x,
              ssm_norm_g.reshape(1, -1), onehot)
    hist = SUBLANES - (CONV_WIDTH - 1)
    y_p, h_p = _ssd_core(zx, dtx, 0, NP // SSD_L, SSD_L, False,
                         jnp.zeros((BATCH, SUBLANES, CONV_DIM), F32),
                         jnp.zeros((BATCH, SSM_STATE, SSM_INNER), F32), *common)
    y_s, h_s = _ssd_core(zx, dtx, NP, DEC_BATCH, DEC_SEQ, True,
                         jnp.pad(state_conv, ((0, 0), (hist, 0), (0, 0))),
                         _state_to_t(state_ssm), *common)
    x = _ssd_out(x, y_p, y_s, w_out_ssd, ln_g[1, 1], ln_b[1, 1])
    x = _ffn(x, ffn_post_up[1], ffn_post_down[1], ln_g[1, 2], ln_b[1, 2])

    tail = CONV_WIDTH - 1
    conv_p = zx[:NP].reshape(BATCH, SEQ, nzx)[:, SEQ - tail:, SSM_INNER:]
    conv_s = zx[NP:].reshape(DEC_BATCH, DEC_SEQ, nzx)[:, DEC_SEQ - tail:, SSM_INNER:]
    ssm_p = _state_from_t(h_p)
    ssm_s = _state_from_t(h_s)

    y_prompt = x[:NP].reshape(BATCH, SEQ, D_MODEL)
    y_sample = x[NP:].reshape(DEC_BATCH, DEC_SEQ, D_MODEL)
    return (y_prompt, y_sample, pool_p, pool_s, ckv_p, ckv_s, kpe_p, kpe_s, conv_p, conv_s, ssm_p, ssm_s)
```

```python
import functools

import jax
import jax.numpy as jnp
import numpy as np
from jax import lax
from jax.experimental import pallas as pl
from jax.experimental.pallas import tpu as pltpu

F32 = jnp.float32
BF16 = jnp.bfloat16

D_MODEL = 2048
BATCH = 1
SEQ = 16384
DEPTH = 2
DEC_BATCH = 16
DEC_SEQ = 64
PAST_LEN = 1024
CHUNK = 64
ALPHA = (2 * DEPTH) ** 0.25
LN_EPS = 1e-5
RMS_EPS = 1e-6
FF_DIM = 5504
POOL_WINDOWS = (2, 4, 8, 16)
POOL_DIM = 512
POOL_GROUP = 128
POOL_STATE = 15
NOPE_DIM = 128
ROPE_DIM = 64
V_DIM = 128
MLA_HEADS = 12
Q_LORA = 512
KV_LORA = 512
ROPE_THETA = 10000.0
SSM_INNER = 4096
SSM_HEAD_DIM = 64
SSM_HEADS = 64
SSM_GROUPS = 8
SSM_HPG = 8
SSM_STATE = 128
CONV_WIDTH = 4
CONV_DIM = SSM_INNER + 2 * SSM_GROUPS * SSM_STATE

NP = BATCH * SEQ
NS = DEC_BATCH * DEC_SEQ
NTOK = NP + NS

LANES = 128
SUBLANES = 8
VMEM_LIMIT_BYTES = 56 * 1024 * 1024

FF_TILE = 512
FF_MAIN_TILES = FF_DIM // FF_TILE
FF_TAIL = FF_DIM - FF_MAIN_TILES * FF_TILE
FFN_TM = 512
TOK_TM = 512
ATT_TQ = 1024
ATT_TK = 512
SSD_L = 128
QK_PAD = 256
SSD_IN_TM = 1024
SSD_IN_TN = 1024
POOL_EXT = 16
ATT_SCALE = float((NOPE_DIM + ROPE_DIM) ** -0.5)
ATT_SCALE_LOG2E = ATT_SCALE * 1.4426950408889634


def _cparams(sem):
    return pltpu.CompilerParams(dimension_semantics=sem, vmem_limit_bytes=VMEM_LIMIT_BYTES)


def _layer_norm(y, g, b):
    mu = jnp.mean(y, axis=-1, keepdims=True)
    d = y - mu
    var = jnp.mean(d * d, axis=-1, keepdims=True)
    return d * lax.rsqrt(var + LN_EPS) * g + b


def _rms(h, g):
    return h * lax.rsqrt(jnp.mean(h * h, axis=-1, keepdims=True) + RMS_EPS) * g


def _silu(x):
    h = 0.5 * x
    return h + h * jnp.tanh(h)


def _dot(a, b):
    return jnp.dot(a, b, preferred_element_type=F32)


def _dot_nt(a, b):
    return lax.dot_general(a, b, (((1,), (1,)), ((), ())), preferred_element_type=F32)


def _dot_tn(a, b):
    return lax.dot_general(a, b, (((0,), (0,)), ((), ())), preferred_element_type=F32)


def _split2(v):
    hi = v.astype(BF16)
    lo = (v - hi.astype(F32)).astype(BF16)
    return hi, lo


def _split3(v):
    hi = v.astype(BF16)
    r = v - hi.astype(F32)
    mid = r.astype(BF16)
    lo = (r - mid.astype(F32)).astype(BF16)
    return hi, mid, lo


def _ffn_kernel(npt, n_in, n_out, *refs):
    x_refs = refs[:n_in]
    wg_ref, wu_ref, wd_ref, wgt_ref, wut_ref, wdt_ref, g_ref, b_ref = refs[n_in:n_in + 8]
    o_refs = refs[n_in + 8:n_in + 8 + n_out]
    xb_ref = refs[n_in + 8 + n_out]
    acc_ref = o_refs[0] if n_out == 1 else refs[n_in + 9 + n_out]
    i = pl.program_id(0)
    j = pl.program_id(1)

    def load_x():
        if n_in == 1:
            return x_refs[0][...]
        return jnp.where(i < npt, x_refs[0][...], x_refs[1][...])

    def partial_down(xb, wg, wu, wd):
        return _dot((_silu(_dot(xb, wg)) * _dot(xb, wu)).astype(BF16), wd)

    @pl.when(j == 0)
    def _():
        xb = load_x().astype(BF16)
        xb_ref[...] = xb
        acc_ref[...] = partial_down(xb, wgt_ref[0], wut_ref[0], wdt_ref[0])

    acc_ref[...] += partial_down(xb_ref[...], wg_ref[...], wu_ref[0], wd_ref[...])

    @pl.when(j == pl.num_programs(1) - 1)
    def _():
        y = _layer_norm(ALPHA * load_x() + 0.5 * acc_ref[...], g_ref[...], b_ref[...])
        if n_out == 1:
            o_refs[0][...] = y
        else:
            @pl.when(i < npt)
            def _():
                o_refs[0][...] = y

            @pl.when(i >= npt)
            def _():
                o_refs[1][...] = y


def _ffn(xs, w_up_b, w_down_b, layer, g, b, split_out=False):
    tm = FFN_TM
    npt = NP // tm
    nff = FF_MAIN_TILES
    tail0 = nff * FF_TILE
    el = pl.Element
    aligned = lambda v: pl.multiple_of(v, LANES)
    n_in = len(xs)
    n_out = 2 if split_out else 1
    prompt_tile = lambda i, j: (jnp.minimum(i, npt - 1), 0)
    sample_tile = lambda i, j: (jnp.maximum(i - npt, 0), 0)
    whole_tile = lambda i, j: (i, 0)
    x_specs = [pl.BlockSpec((tm, D_MODEL), m) for m in ((whole_tile,) if n_in == 1 else (prompt_tile, sample_tile))]
    if split_out:
        out_specs = [pl.BlockSpec((tm, D_MODEL), prompt_tile), pl.BlockSpec((tm, D_MODEL), sample_tile)]
        out_shape = [jax.ShapeDtypeStruct((NP, D_MODEL), F32), jax.ShapeDtypeStruct((NS, D_MODEL), F32)]
        scratch = [pltpu.VMEM((tm, D_MODEL), BF16), pltpu.VMEM((tm, D_MODEL), F32)]
    else:
        out_specs = pl.BlockSpec((tm, D_MODEL), whole_tile)
        out_shape = jax.ShapeDtypeStruct((NTOK, D_MODEL), F32)
        scratch = [pltpu.VMEM((tm, D_MODEL), BF16)]
    return pl.pallas_call(
        functools.partial(_ffn_kernel, npt, n_in, n_out),
        grid=(NTOK // tm, nff),
        in_specs=x_specs + [
            pl.BlockSpec((None, D_MODEL, FF_TILE), lambda i, j: (layer, 0, j)),
            pl.BlockSpec((el(1), el(D_MODEL), el(FF_TILE)), lambda i, j: (layer, 0, aligned(FF_DIM + j * FF_TILE))),
            pl.BlockSpec((None, FF_TILE, D_MODEL), lambda i, j: (layer, j, 0)),
            pl.BlockSpec((el(1), el(D_MODEL), el(FF_TAIL)), lambda i, j: (layer, 0, tail0), pipeline_mode=pl.Buffered(1)),
            pl.BlockSpec((el(1), el(D_MODEL), el(FF_TAIL)), lambda i, j: (layer, 0, FF_DIM + tail0), pipeline_mode=pl.Buffered(1)),
            pl.BlockSpec((el(1), el(FF_TAIL), el(D_MODEL)), lambda i, j: (layer, tail0, 0), pipeline_mode=pl.Buffered(1)),
            pl.BlockSpec((1, D_MODEL), lambda i, j: (0, 0)),
            pl.BlockSpec((1, D_MODEL), lambda i, j: (0, 0)),
        ],
        out_specs=out_specs,
        out_shape=out_shape,
        scratch_shapes=scratch,
        compiler_params=_cparams(("arbitrary", "arbitrary")),
        name="ffn_postnorm",
    )(*xs, w_up_b, w_up_b, w_down_b, w_up_b, w_up_b, w_down_b, g.reshape(1, -1), b.reshape(1, -1))


def _pool_window_mean_minus(ext_ref, base, rows, pos0):
    first = base + POOL_EXT
    t = lax.broadcasted_iota(jnp.int32, (rows, POOL_GROUP), 0)
    posp1 = (pos0 + t + 1).astype(F32)
    outs = []
    for g, w in enumerate(POOL_WINDOWS):
        cols = slice(g * POOL_GROUP, (g + 1) * POOL_GROUP)
        cur = ext_ref[pl.ds(first, rows), cols]
        tot = cur
        for s in range(1, w):
            tot = tot + ext_ref[pl.ds(first - s, rows), cols]
        outs.append(tot / jnp.minimum(posp1, float(w)) - cur)
    return outs


def _ab_in_kernel(n_prompt_tiles, x_ref, w_ref, past_ref, pw_ref, ps_ref, qg_ref, kg_ref, rope_ref,
                  ut_ref, yp_ref, qn_ref, ckvp_ref, ckvs_ref, kpep_ref, kpes_ref, ext_ref):
    i = pl.program_id(0)
    tm = x_ref.shape[0]
    h = _dot(x_ref[...].astype(BF16), w_ref[...])
    u = h[:, :POOL_DIM]
    for s in range(tm // DEC_SEQ):
        ut_ref[s] = u[(s + 1) * DEC_SEQ - POOL_EXT:(s + 1) * DEC_SEQ, :]
    qn_ref[...] = _rms(h[:, POOL_DIM:POOL_DIM + Q_LORA], qg_ref[...]).astype(BF16)
    ckv = _rms(h[:, POOL_DIM + Q_LORA:POOL_DIM + Q_LORA + KV_LORA], kg_ref[...])
    kv = h[:, POOL_DIM + Q_LORA + KV_LORA:] * rope_ref[...]
    kpe = (kv + pltpu.roll(kv, ROPE_DIM, axis=1))[:, :ROPE_DIM]

    @pl.when(i < n_prompt_tiles)
    def _():
        ckvp_ref[...] = ckv
        kpep_ref[...] = kpe

    @pl.when(i >= n_prompt_tiles)
    def _():
        ckvs_ref[...] = ckv
        kpes_ref[...] = kpe

    def finish(d_groups):
        for g in range(len(POOL_WINDOWS)):
            cols = slice(g * POOL_GROUP, (g + 1) * POOL_GROUP)
            y = _dot(d_groups[g].astype(BF16), pw_ref[g]) * ps_ref[:, cols]
            yp_ref[:, cols] = y.astype(BF16)

    @pl.when(i < n_prompt_tiles)
    def _():
        @pl.when(i == 0)
        def _():
            ext_ref[pl.ds(0, POOL_EXT), :] = jnp.zeros((POOL_EXT, POOL_DIM), F32)

        ext_ref[pl.ds(POOL_EXT, tm), :] = u
        finish(_pool_window_mean_minus(ext_ref, 0, tm, i * tm))
        ext_ref[pl.ds(0, POOL_EXT), :] = u[tm - POOL_EXT:, :]

    @pl.when(i >= n_prompt_tiles)
    def _():
        nseg = tm // DEC_SEQ
        stride = POOL_EXT + DEC_SEQ
        parts = [[] for _ in POOL_WINDOWS]
        for s in range(nseg):
            ext_ref[pl.ds(s * stride, POOL_EXT), :] = past_ref[s]
            ext_ref[pl.ds(s * stride + POOL_EXT, DEC_SEQ), :] = u[s * DEC_SEQ:(s + 1) * DEC_SEQ, :]
        for s in range(nseg):
            d = _pool_window_mean_minus(ext_ref, s * stride, DEC_SEQ, PAST_LEN)
            for g in range(len(POOL_WINDOWS)):
                parts[g].append(d[g])
        finish([jnp.concatenate(p, axis=0) for p in parts])


def _ab_in(x, w_in_x, pool_past, pool_w, pool_scale, q_norm_g, kv_norm_g, rope_k):
    n = x.shape[0]
    tm = TOK_TM
    nseg = tm // DEC_SEQ
    npt = NP // tm
    wcols = w_in_x.shape[1]
    ext_rows = max(tm + POOL_EXT, nseg * (POOL_EXT + DEC_SEQ))
    const = lambda i: (0, 0)
    prompt_tile = lambda i: (jnp.minimum(i, npt - 1), 0)
    sample_tile = lambda i: (jnp.maximum(i - npt, 0), 0)
    return pl.pallas_call(
        functools.partial(_ab_in_kernel, npt),
        grid=(n // tm,),
        in_specs=[
            pl.BlockSpec((tm, D_MODEL), lambda i: (i, 0)),
            pl.BlockSpec((D_MODEL, wcols), const),
            pl.BlockSpec((nseg, POOL_EXT, POOL_DIM), lambda i: (jnp.maximum(i - npt, 0), 0, 0)),
            pl.BlockSpec((len(POOL_WINDOWS), POOL_GROUP, POOL_GROUP), lambda i: (0, 0, 0)),
            pl.BlockSpec((1, POOL_DIM), const),
            pl.BlockSpec((1, Q_LORA), const),
            pl.BlockSpec((1, KV_LORA), const),
            pl.BlockSpec((tm, LANES), lambda i: (i, 0)),
        ],
        out_specs=[
            pl.BlockSpec((nseg, POOL_EXT, POOL_DIM), lambda i: (i, 0, 0)),
            pl.BlockSpec((tm, POOL_DIM), lambda i: (i, 0)),
            pl.BlockSpec((tm, Q_LORA), lambda i: (i, 0)),
            pl.BlockSpec((tm, KV_LORA), prompt_tile),
            pl.BlockSpec((tm, KV_LORA), sample_tile),
            pl.BlockSpec((tm, ROPE_DIM), prompt_tile),
            pl.BlockSpec((tm, ROPE_DIM), sample_tile),
        ],
        out_shape=[
            jax.ShapeDtypeStruct((n // DEC_SEQ, POOL_EXT, POOL_DIM), F32),
            jax.ShapeDtypeStruct((n, POOL_DIM), BF16),
            jax.ShapeDtypeStruct((n, Q_LORA), BF16),
            jax.ShapeDtypeStruct((NP, KV_LORA), F32),
            jax.ShapeDtypeStruct((NS, KV_LORA), F32),
            jax.ShapeDtypeStruct((NP, ROPE_DIM), F32),
            jax.ShapeDtypeStruct((NS, ROPE_DIM), F32),
        ],
        scratch_shapes=[pltpu.VMEM((ext_rows, POOL_DIM), F32)],
        compiler_params=_cparams(("arbitrary",)),
        name="ab_in",
    )(x, w_in_x, pool_past, pool_w.astype(BF16), pool_scale.reshape(1, -1),
      q_norm_g.reshape(1, -1), kv_norm_g.reshape(1, -1), rope_k)


def _q_up_kernel(qn_ref, w_ref, rope_ref, q_ref):
    qn = qn_ref[...]
    rope = rope_ref[...]
    lane = lax.broadcasted_iota(jnp.int32, rope.shape, 1)
    hpc = 4
    ncol = hpc * LANES
    for c in range(MLA_HEADS // hpc):
        nope = _dot(qn, w_ref[:, c * ncol:(c + 1) * ncol])
        rp = _dot(qn, w_ref[:, MLA_HEADS * LANES + c * ncol:MLA_HEADS * LANES + (c + 1) * ncol])
        for hh in range(hpc):
            hd = c * hpc + hh
            v = rp[:, hh * LANES:(hh + 1) * LANES] * rope
            rot = jnp.where(lane < ROPE_DIM, v + pltpu.roll(v, ROPE_DIM, axis=1), 0.0)
            q_ref[hd, :, 0:LANES] = (nope[:, hh * LANES:(hh + 1) * LANES] * ATT_SCALE).astype(BF16)
            q_ref[hd, :, LANES:QK_PAD] = (rot * ATT_SCALE).astype(BF16)


def _q_up(qn, wq_x, rope_q, row0, nrows):
    tm = TOK_TM
    t0 = row0 // tm
    return pl.pallas_call(
        _q_up_kernel,
        grid=(nrows // tm,),
        in_specs=[
            pl.BlockSpec((tm, Q_LORA), lambda i: (t0 + i, 0)),
            pl.BlockSpec(wq_x.shape, lambda i: (0, 0)),
            pl.BlockSpec((tm, LANES), lambda i: (t0 + i, 0)),
        ],
        out_specs=pl.BlockSpec((MLA_HEADS, tm, QK_PAD), lambda i: (0, i, 0)),
        out_shape=jax.ShapeDtypeStruct((MLA_HEADS, nrows, QK_PAD), BF16),
        compiler_params=_cparams(("parallel",)),
        name="q_up",
    )(qn, wq_x, rope_q)


def _q_up_t_kernel(qn_ref, wt_ref, ropet_ref, qt_ref):
    qt_all = _dot_nt(wt_ref[...], qn_ref[...])
    ropet = ropet_ref[...]
    nrope = MLA_HEADS * LANES
    zeros = jnp.zeros((QK_PAD - LANES - ROPE_DIM, qt_all.shape[1]), BF16)
    for hd in range(MLA_HEADS):
        v = qt_all[nrope + hd * LANES:nrope + (hd + 1) * LANES, :] * ropet
        rot = v[:ROPE_DIM] + v[ROPE_DIM:]
        qt_ref[hd, 0, 0:LANES, :] = (qt_all[hd * LANES:(hd + 1) * LANES, :] * ATT_SCALE_LOG2E).astype(BF16)
        qt_ref[hd, 0, LANES:LANES + ROPE_DIM, :] = (rot * ATT_SCALE_LOG2E).astype(BF16)
        qt_ref[hd, 0, LANES + ROPE_DIM:QK_PAD, :] = zeros


def _q_up_t(qn, wq_xt, rope_t, nrows):
    tm = ATT_TQ
    return pl.pallas_call(
        _q_up_t_kernel,
        grid=(nrows // tm,),
        in_specs=[
            pl.BlockSpec((tm, Q_LORA), lambda i: (i, 0)),
            pl.BlockSpec(wq_xt.shape, lambda i: (0, 0)),
            pl.BlockSpec((LANES, tm), lambda i: (0, i)),
        ],
        out_specs=pl.BlockSpec((MLA_HEADS, 1, QK_PAD, tm), lambda i: (0, i, 0, 0)),
        out_shape=jax.ShapeDtypeStruct((MLA_HEADS, nrows // tm, QK_PAD, tm), BF16),
        compiler_params=_cparams(("parallel",)),
        name="q_up_t",
    )(qn, wq_xt, rope_t)


def _kv_up_kernel(ckv_ref, kpe_ref, w_ref, k_ref, v_ref):
    c = ckv_ref[...].astype(BF16)
    kpe = kpe_ref[...].astype(BF16)
    hpc = 4
    ncol = hpc * LANES
    for j in range(MLA_HEADS // hpc):
        kn = _dot(c, w_ref[:, j * ncol:(j + 1) * ncol])
        vv = _dot(c, w_ref[:, MLA_HEADS * LANES + j * ncol:MLA_HEADS * LANES + (j + 1) * ncol])
        for hh in range(hpc):
            hd = j * hpc + hh
            k_ref[hd, :, 0:LANES] = kn[:, hh * LANES:(hh + 1) * LANES].astype(BF16)
            k_ref[hd, :, LANES:LANES + ROPE_DIM] = kpe
            k_ref[hd, :, LANES + ROPE_DIM:QK_PAD] = jnp.zeros_like(kpe)
            v_ref[hd] = vv[:, hh * LANES:(hh + 1) * LANES].astype(BF16)


def _kv_up(ckv, kpe, wkv_x):
    tm = TOK_TM
    nrows = ckv.shape[0]
    return pl.pallas_call(
        _kv_up_kernel,
        grid=(nrows // tm,),
        in_specs=[
            pl.BlockSpec((tm, KV_LORA), lambda i: (i, 0)),
            pl.BlockSpec((tm, ROPE_DIM), lambda i: (i, 0)),
            pl.BlockSpec(wkv_x.shape, lambda i: (0, 0)),
        ],
        out_specs=[
            pl.BlockSpec((MLA_HEADS, tm, QK_PAD), lambda i: (0, i, 0)),
            pl.BlockSpec((MLA_HEADS, tm, V_DIM), lambda i: (0, i, 0)),
        ],
        out_shape=[
            jax.ShapeDtypeStruct((MLA_HEADS, nrows, QK_PAD), BF16),
            jax.ShapeDtypeStruct((MLA_HEADS, nrows, V_DIM), BF16),
        ],
        compiler_params=_cparams(("parallel",)),
        name="kv_up",
    )(ckv, kpe, wkv_x)


def _kv_up_t_kernel(ckv_ref, kpe_ref, wk_ref, wvt_ref, k_ref, vt_ref):
    c = ckv_ref[...].astype(BF16)
    kpe = kpe_ref[...].astype(BF16)
    hpc = 4
    ncol = hpc * LANES
    for j in range(MLA_HEADS // hpc):
        kn = _dot(c, wk_ref[:, j * ncol:(j + 1) * ncol])
        for hh in range(hpc):
            hd = j * hpc + hh
            k_ref[hd, :, 0:LANES] = kn[:, hh * LANES:(hh + 1) * LANES].astype(BF16)
            k_ref[hd, :, LANES:LANES + ROPE_DIM] = kpe
            k_ref[hd, :, LANES + ROPE_DIM:QK_PAD] = jnp.zeros_like(kpe)
    vt_all = _dot_nt(wvt_ref[...], c)
    for hd in range(MLA_HEADS):
        vt_ref[hd, 0] = vt_all[hd * V_DIM:(hd + 1) * V_DIM, :].astype(BF16)


def _kv_up_t(ckv, kpe, wk, wvt):
    tm = ATT_TK
    nrows = ckv.shape[0]
    return pl.pallas_call(
        _kv_up_t_kernel,
        grid=(nrows // tm,),
        in_specs=[
            pl.BlockSpec((tm, KV_LORA), lambda i: (i, 0)),
            pl.BlockSpec((tm, ROPE_DIM), lambda i: (i, 0)),
            pl.BlockSpec(wk.shape, lambda i: (0, 0)),
            pl.BlockSpec(wvt.shape, lambda i: (0, 0)),
        ],
        out_specs=[
            pl.BlockSpec((MLA_HEADS, tm, QK_PAD), lambda i: (0, i, 0)),
            pl.BlockSpec((MLA_HEADS, 1, V_DIM, tm), lambda i: (0, i, 0, 0)),
        ],
        out_shape=[
            jax.ShapeDtypeStruct((MLA_HEADS, nrows, QK_PAD), BF16),
            jax.ShapeDtypeStruct((MLA_HEADS, nrows // tm, V_DIM, tm), BF16),
        ],
        compiler_params=_cparams(("parallel",)),
        name="kv_up_t",
    )(ckv, kpe, wk, wvt)


def _attn_prompt_kernel(qt_ref, k_ref, vt_ref, o_ref, s0_ref, s1_ref, m_ref, l_ref, acc_ref):
    i = pl.program_id(1)
    tk, tq = s0_ref.shape

    def scores(c, s_ref):
        start = pl.multiple_of(c * tk, tk)
        s_ref[...] = _dot(k_ref[0, pl.ds(start, tk), :], qt_ref[0, 0])

    def softmax_pv(c, s_ref, mask):
        st = s_ref[...]
        if mask is not None:
            st = jnp.where(mask, st, -jnp.inf)
        m = m_ref[...]
        m_new = jnp.maximum(m, jnp.max(st, axis=0, keepdims=True))
        alpha = jnp.exp2(m - m_new)
        p = jnp.exp2(st - m_new)
        l_ref[...] = alpha * l_ref[...] + jnp.sum(p, axis=0, keepdims=True)
        m_ref[...] = m_new
        acc_ref[...] = alpha * acc_ref[...] + _dot(vt_ref[0, c], p.astype(BF16))

    m_ref[...] = jnp.full(m_ref.shape, -jnp.inf, F32)
    l_ref[...] = jnp.zeros(l_ref.shape, F32)
    acc_ref[...] = jnp.zeros(acc_ref.shape, F32)
    key_chunk = lax.broadcasted_iota(jnp.int32, (tk, tq), 0) // CHUNK
    query_chunk = lax.broadcasted_iota(jnp.int32, (tk, tq), 1) // CHUNK

    scores(0, s0_ref)

    def pair(j, carry):
        scores(2 * j + 1, s1_ref)
        softmax_pv(2 * j, s0_ref, None)
        scores(2 * j + 2, s0_ref)
        softmax_pv(2 * j + 1, s1_ref, None)
        return carry

    lax.fori_loop(0, i, pair, 0)
    scores(2 * i + 1, s1_ref)
    softmax_pv(2 * i, s0_ref, query_chunk >= key_chunk)
    softmax_pv(2 * i + 1, s1_ref, query_chunk >= key_chunk + tk // CHUNK)

    o_ref[...] = (acc_ref[...] / l_ref[...]).T.astype(BF16)


def _attn_prompt(qt, k, vt):
    tq, tk = ATT_TQ, ATT_TK
    assert tq == 2 * tk and tk % CHUNK == 0
    return pl.pallas_call(
        _attn_prompt_kernel,
        grid=(MLA_HEADS, NP // tq),
        in_specs=[
            pl.BlockSpec((1, 1, QK_PAD, tq), lambda h, i: (h, i, 0, 0)),
            pl.BlockSpec((1, NP, QK_PAD), lambda h, i: (h, 0, 0)),
            pl.BlockSpec((1, NP // tk, V_DIM, tk), lambda h, i: (h, 0, 0, 0)),
        ],
        out_specs=pl.BlockSpec((tq, V_DIM), lambda h, i: (i, h)),
        out_shape=jax.ShapeDtypeStruct((NP, MLA_HEADS * V_DIM), BF16),
        scratch_shapes=[
            pltpu.VMEM((tk, tq), F32), pltpu.VMEM((tk, tq), F32),
            pltpu.VMEM((1, tq), F32), pltpu.VMEM((1, tq), F32), pltpu.VMEM((V_DIM, tq), F32),
        ],
        compiler_params=_cparams(("parallel", "arbitrary")),
        name="attn_prompt",
    )(qt, k, vt)


def _attn_sample_kernel(q_ref, kp_ref, vp_ref, kn_ref, vn_ref, o_ref):
    for hd in range(MLA_HEADS):
        q = q_ref[hd]
        sp = _dot_nt(q, kp_ref[hd])
        sn = _dot_nt(q, kn_ref[hd])
        m = jnp.maximum(jnp.max(sp, axis=1, keepdims=True), jnp.max(sn, axis=1, keepdims=True))
        pp = jnp.exp(sp - m)
        pn = jnp.exp(sn - m)
        l = jnp.sum(pp, axis=1, keepdims=True) + jnp.sum(pn, axis=1, keepdims=True)
        o = _dot(pp.astype(BF16), vp_ref[hd]) + _dot(pn.astype(BF16), vn_ref[hd])
        o_ref[:, hd * V_DIM:(hd + 1) * V_DIM] = (o / l).astype(BF16)


def _attn_sample(q, k_past, v_past, k_new, v_new):
    assert PAST_LEN % CHUNK == 0 and DEC_SEQ <= CHUNK
    return pl.pallas_call(
        _attn_sample_kernel,
        grid=(DEC_BATCH,),
        in_specs=[
            pl.BlockSpec((MLA_HEADS, DEC_SEQ, QK_PAD), lambda b: (0, b, 0)),
            pl.BlockSpec((MLA_HEADS, PAST_LEN, QK_PAD), lambda b: (0, b, 0)),
            pl.BlockSpec((MLA_HEADS, PAST_LEN, V_DIM), lambda b: (0, b, 0)),
            pl.BlockSpec((MLA_HEADS, DEC_SEQ, QK_PAD), lambda b: (0, b, 0)),
            pl.BlockSpec((MLA_HEADS, DEC_SEQ, V_DIM), lambda b: (0, b, 0)),
        ],
        out_specs=pl.BlockSpec((DEC_SEQ, MLA_HEADS * V_DIM), lambda b: (b, 0)),
        out_shape=jax.ShapeDtypeStruct((NS, MLA_HEADS * V_DIM), BF16),
        compiler_params=_cparams(("parallel",)),
        name="attn_sample",
    )(q, k_past, v_past, k_new, v_new)


def _ab_out_kernel(npt, x_ref, yp_ref, ap_ref, as_ref, wp_ref, wa_ref, g_ref, b_ref, o_ref):
    i = pl.program_id(0)
    att = jnp.where(i < npt, ap_ref[...], as_ref[...])
    mix = _dot(yp_ref[...], wp_ref[...]) + _dot(att, wa_ref[...])
    o_ref[...] = _layer_norm(ALPHA * x_ref[...] + mix, g_ref[...], b_ref[...])


def _ab_out(x, ypool, att_p, att_s, w_out, g, b):
    n = x.shape[0]
    tm = TOK_TM
    npt = NP // tm
    w = w_out.astype(BF16)
    adim = MLA_HEADS * V_DIM
    const = lambda i: (0, 0)
    return pl.pallas_call(
        functools.partial(_ab_out_kernel, npt),
        grid=(n // tm,),
        in_specs=[
            pl.BlockSpec((tm, D_MODEL), lambda i: (i, 0)),
            pl.BlockSpec((tm, POOL_DIM), lambda i: (i, 0)),
            pl.BlockSpec((tm, adim), lambda i: (jnp.minimum(i, npt - 1), 0)),
            pl.BlockSpec((tm, adim), lambda i: (jnp.maximum(i - npt, 0), 0)),
            pl.BlockSpec((POOL_DIM, D_MODEL), const),
            pl.BlockSpec((adim, D_MODEL), const),
            pl.BlockSpec((1, D_MODEL), const),
            pl.BlockSpec((1, D_MODEL), const),
        ],
        out_specs=pl.BlockSpec((tm, D_MODEL), lambda i: (i, 0)),
        out_shape=jax.ShapeDtypeStruct((n, D_MODEL), F32),
        compiler_params=_cparams(("parallel",)),
        name="ab_out",
    )(x, ypool, att_p, att_s, w[:POOL_DIM], w[POOL_DIM:], g.reshape(1, -1), b.reshape(1, -1))


def _ssd_out_kernel(npt, x_ref, ap_ref, as_ref, w_ref, g_ref, b_ref, o_ref):
    act = jnp.where(pl.program_id(0) < npt, ap_ref[...], as_ref[...])
    o_ref[...] = _layer_norm(ALPHA * x_ref[...] + _dot(act, w_ref[...]), g_ref[...], b_ref[...])


def _ssd_out(x, act_p, act_s, w_out, g, b):
    n = x.shape[0]
    tm = TOK_TM
    npt = NP // tm
    const = lambda i: (0, 0)
    return pl.pallas_call(
        functools.partial(_ssd_out_kernel, npt),
        grid=(n // tm,),
        in_specs=[
            pl.BlockSpec((tm, D_MODEL), lambda i: (i, 0)),
            pl.BlockSpec((tm, SSM_INNER), lambda i: (jnp.minimum(i, npt - 1), 0)),
            pl.BlockSpec((tm, SSM_INNER), lambda i: (jnp.maximum(i - npt, 0), 0)),
            pl.BlockSpec((SSM_INNER, D_MODEL), const),
            pl.BlockSpec((1, D_MODEL), const),
            pl.BlockSpec((1, D_MODEL), const),
        ],
        out_specs=pl.BlockSpec((tm, D_MODEL), lambda i: (i, 0)),
        out_shape=jax.ShapeDtypeStruct((n, D_MODEL), F32),
        compiler_params=_cparams(("parallel",)),
        name="ssd_out",
    )(x, act_p, act_s, w_out.astype(BF16), g.reshape(1, -1), b.reshape(1, -1))


def _ssd_in_kernel(x_ref, w_ref, wdt_ref, o_ref, dt_ref, xb_ref):
    @pl.when(pl.program_id(1) == 0)
    def _():
        xb_ref[...] = x_ref[...].astype(BF16)
        dt_ref[...] = _dot(xb_ref[...], wdt_ref[...])

    o_ref[...] = _dot(xb_ref[...], w_ref[...])


def _ssd_in(x, w_all, w_dt):
    n = x.shape[0]
    tm = SSD_IN_TM
    ncol = SSM_INNER + CONV_DIM
    w_zx = w_all
    return pl.pallas_call(
        _ssd_in_kernel,
        grid=(n // tm, ncol // SSD_IN_TN),
        in_specs=[
            pl.BlockSpec((tm, D_MODEL), lambda i, j: (i, 0)),
            pl.BlockSpec((D_MODEL, SSD_IN_TN), lambda i, j: (0, j)),
            pl.BlockSpec((D_MODEL, LANES), lambda i, j: (0, 0)),
        ],
        out_specs=[
            pl.BlockSpec((tm, SSD_IN_TN), lambda i, j: (i, j)),
            pl.BlockSpec((tm, LANES), lambda i, j: (i, 0)),
        ],
        out_shape=[
            jax.ShapeDtypeStruct((n, ncol), F32),
            jax.ShapeDtypeStruct((n, LANES), F32),
        ],
        scratch_shapes=[pltpu.VMEM((tm, D_MODEL), BF16)],
        compiler_params=_cparams(("parallel", "arbitrary")),
        name="ssd_in",
    )(x, w_zx, w_dt)


def _ssd_kernel(rows, fresh_each_step, z_ref, xs_ref, bc_ref, dt_ref, cpast_ref, h0_ref, cw_ref,
                cb_ref, dtb_ref, a_ref, dsk_ref, ng_ref, e_ref, y_ref, h_ref, ext_ref):
    c = pl.program_id(0)
    L = SSD_L
    G, R, P, S = SSM_GROUPS, SSM_HPG, SSM_HEAD_DIM, SSM_STATE
    halo = SUBLANES

    def start_sequence():
        ext_ref[pl.ds(0, halo), :] = cpast_ref[0]
        h_ref[0] = h0_ref[0]

    if fresh_each_step:
        start_sequence()
    else:
        pl.when(c == 0)(start_sequence)

    ext_ref[pl.ds(halo, rows), 0:SSM_INNER] = xs_ref[...]
    ext_ref[pl.ds(halo, rows), SSM_INNER:CONV_DIM] = bc_ref[...]
    if rows < L:
        ext_ref[pl.ds(halo + rows, L - rows), :] = jnp.zeros((L - rows, CONV_DIM), F32)

    win = ext_ref[pl.ds(0, halo + L), :]
    conv = cb_ref[...] + cw_ref[CONV_WIDTH - 1:CONV_WIDTH, :] * win[halo:]
    for back in range(1, CONV_WIDTH):
        k = CONV_WIDTH - 1 - back
        conv = conv + cw_ref[k:k + 1, :] * pltpu.roll(win, back, axis=0)[halo:]
    ext_ref[pl.ds(0, halo), :] = ext_ref[pl.ds(rows, halo), :]
    act = _silu(conv)
    xs = act[:, :SSM_INNER]
    xsb = xs.astype(BF16)
    bmat = act[:, SSM_INNER:SSM_INNER + G * S].astype(BF16)
    cmat = act[:, SSM_INNER + G * S:].astype(BF16)

    dtr = dt_ref[...] + dtb_ref[...]
    dt = jnp.maximum(dtr, 0.0) + jnp.log1p(jnp.exp(-jnp.abs(dtr)))
    if rows < L:
        dt = jnp.concatenate([dt, jnp.zeros((L - rows, LANES), F32)], axis=0)
    da = dt * a_ref[...]
    ti = lax.broadcasted_iota(jnp.int32, (L, L), 0)
    si = lax.broadcasted_iota(jnp.int32, (L, L), 1)
    causal = ti >= si
    tri = jnp.where(causal, 1.0, 0.0).astype(BF16)
    acs = sum(_dot(tri, part) for part in _split3(da))
    acs_t = acs.T
    dt_t = dt.T
    last = acs[L - 1:L, :]
    to_end = jnp.exp(last - acs) * dt
    eacs = jnp.exp(acs)
    onehot = e_ref[...]
    expand = lambda v: sum(_dot(part, onehot) for part in _split2(v))
    x_end = (xs * expand(to_end)).astype(BF16)
    eacs_x = expand(eacs)
    dec_x = sum(_dot(part, onehot) for part in _split3(jnp.broadcast_to(jnp.exp(last), (SUBLANES, LANES))))[0:1, :]

    lane4 = lax.broadcasted_iota(jnp.int32, (L, 4 * P), 1) // P
    for g in range(G):
        bg = bmat[:, g * S:(g + 1) * S]
        cg = cmat[:, g * S:(g + 1) * S]
        cbm = _dot_nt(cg, bg)
        cols = slice(g * R * P, (g + 1) * R * P)
        ht = h_ref[0, :, cols]
        y_state = _dot(cg, ht.astype(BF16)) * eacs_x[:, cols]
        y_parts = []
        for half in range(R // 4):
            x4 = xsb[:, g * R * P + half * 4 * P:g * R * P + (half + 1) * 4 * P]
            x_bd = jnp.concatenate([jnp.where(lane4 == r, x4, jnp.zeros_like(x4)) for r in range(4)], axis=0)
            wts = []
            for r in range(4):
                hd = g * R + half * 4 + r
                seg = acs[:, hd:hd + 1] - acs_t[hd:hd + 1, :]
                decay = jnp.exp(jnp.where(causal, seg, -jnp.inf))
                wts.append((cbm * decay * dt_t[hd:hd + 1, :]).astype(BF16))
            y_parts.append(_dot(jnp.concatenate(wts, axis=1), x_bd))
        y = jnp.concatenate(y_parts, axis=1) + y_state + dsk_ref[:, cols] * xs[:, cols]
        gz = y[:rows] * _silu(z_ref[:, cols])
        gz = gz * lax.rsqrt(jnp.mean(gz * gz, axis=-1, keepdims=True) + RMS_EPS)
        y_ref[:, cols] = (gz * ng_ref[:, cols]).astype(BF16)
        h_ref[0, :, cols] = ht * dec_x[:, cols] + _dot_tn(bg, x_end[:, cols])


def _ssd_core(zx, dtx, row0, nblk, rows, fresh_each_step, conv_past8, h0_t, conv_w8, conv_b, dt_bias, a_neg,
              d_skip_x, norm_g, onehot):
    L = SSD_L
    r0 = row0 // rows
    nseq = nblk if fresh_each_step else 1
    seq = (lambda c: c) if fresh_each_step else (lambda c: 0)
    const = lambda c: (0, 0)
    return pl.pallas_call(
        functools.partial(_ssd_kernel, rows, fresh_each_step),
        grid=(nblk,),
        in_specs=[
            pl.BlockSpec((rows, SSM_INNER), lambda c: (r0 + c, 0)),
            pl.BlockSpec((rows, SSM_INNER), lambda c: (r0 + c, 1)),
            pl.BlockSpec((rows, 2 * SSM_GROUPS * SSM_STATE), lambda c: (r0 + c, 4)),
            pl.BlockSpec((rows, LANES), lambda c: (r0 + c, 0)),
            pl.BlockSpec((1, SUBLANES, CONV_DIM), lambda c: (seq(c), 0, 0)),
            pl.BlockSpec((1, SSM_STATE, SSM_INNER), lambda c: (seq(c), 0, 0)),
            pl.BlockSpec((SUBLANES, CONV_DIM), const),
            pl.BlockSpec((1, CONV_DIM), const),
            pl.BlockSpec((1, LANES), const),
            pl.BlockSpec((1, LANES), const),
            pl.BlockSpec((1, SSM_INNER), const),
            pl.BlockSpec((1, SSM_INNER), const),
            pl.BlockSpec((LANES, SSM_INNER), const),
        ],
        out_specs=[
            pl.BlockSpec((rows, SSM_INNER), lambda c: (c, 0)),
            pl.BlockSpec((1, SSM_STATE, SSM_INNER), lambda c: (seq(c), 0, 0)),
        ],
        out_shape=[
            jax.ShapeDtypeStruct((nblk * rows, SSM_INNER), BF16),
            jax.ShapeDtypeStruct((nseq, SSM_STATE, SSM_INNER), F32),
        ],
        scratch_shapes=[pltpu.VMEM((L + 2 * SUBLANES, CONV_DIM), F32)],
        compiler_params=_cparams(("arbitrary",)),
        name="ssd_core_seq" if not fresh_each_step else "ssd_core_blocks",
    )(zx, zx, zx, dtx, conv_past8, h0_t, conv_w8, conv_b, dt_bias, a_neg, d_skip_x, norm_g, onehot)


def _rope_table():
    half = ROPE_DIM // 2
    inv = (1.0 / (np.float32(ROPE_THETA) ** (np.arange(half, dtype=np.float32) * np.float32(2.0) / np.float32(ROPE_DIM)))).astype(np.float32)
    pos = np.concatenate([
        np.tile(np.arange(SEQ, dtype=np.float32), BATCH),
        np.tile(np.arange(DEC_SEQ, dtype=np.float32) + np.float32(PAST_LEN), DEC_BATCH)])
    ang = (pos[:, None] * inv[None, :]).astype(np.float32).astype(np.float64)
    cos, sin = np.cos(ang), np.sin(ang)
    return np.concatenate([cos, cos, -sin, sin], axis=1).astype(np.float32)


def _swap_halves(w):
    half = w.shape[-1] // 2
    return jnp.concatenate([w[..., half:], w[..., :half]], axis=-1)


def _prep_ab_weights(w_in_ab, w_q_up, w_kv_up):
    kpe_w = w_in_ab[:, POOL_DIM + Q_LORA + KV_LORA:]
    w_in_x = jnp.concatenate([w_in_ab, _swap_halves(kpe_w)], axis=1).astype(BF16)
    wq = w_q_up.reshape(Q_LORA, MLA_HEADS, NOPE_DIM + ROPE_DIM)
    wq_rope = wq[..., NOPE_DIM:]
    wq_x = jnp.concatenate([
        wq[..., :NOPE_DIM].reshape(Q_LORA, -1),
        jnp.concatenate([wq_rope, _swap_halves(wq_rope)], axis=-1).reshape(Q_LORA, -1)], axis=1).astype(BF16)
    wkv = w_kv_up.reshape(KV_LORA, MLA_HEADS, NOPE_DIM + V_DIM)
    wk = wkv[..., :NOPE_DIM].reshape(KV_LORA, -1).astype(BF16)
    wv = wkv[..., NOPE_DIM:].reshape(KV_LORA, -1).astype(BF16)
    return w_in_x, wq_x, wk, wv


def _state_to_t(h):
    b = h.shape[0]
    return jnp.transpose(h, (0, 3, 1, 2)).reshape(b, SSM_STATE, SSM_INNER)


def _state_from_t(ht):
    b = ht.shape[0]
    return jnp.transpose(ht.reshape(b, SSM_STATE, SSM_HEADS, SSM_HEAD_DIM), (0, 2, 3, 1))


def kernel(x_prompt, x_sample, state_pool, cache_ckv, cache_kpe, state_conv, state_ssm, ffn_pre_up, ffn_pre_down, ffn_post_up, ffn_post_down, ln_g, ln_b, w_in_ab, pool_w, pool_scale, q_norm_g, w_q_up, kv_norm_g, w_kv_up, w_out_ab, w_in_ssd, conv_w, conv_b, dt_bias, a_log, d_skip, ssm_norm_g, w_out_ssd):
    pre_up, pre_down = ffn_pre_up.astype(BF16), ffn_pre_down.astype(BF16)
    post_up, post_down = ffn_post_up.astype(BF16), ffn_post_down.astype(BF16)

    x = _ffn((x_prompt.reshape(NP, D_MODEL), x_sample.reshape(NS, D_MODEL)), pre_up, pre_down, 0,
             ln_g[0, 0], ln_b[0, 0])
    rope = _rope_table()
    w_in_x, wq_x, wk, wv = _prep_ab_weights(w_in_ab, w_q_up, w_kv_up)
    wkv_x = jnp.concatenate([wk, wv], axis=1)
    pool_past = jnp.pad(state_pool, ((0, 0), (POOL_EXT - POOL_STATE, 0), (0, 0)))
    utail, ypool, qn, ckv_p, ckv_s, kpe_p, kpe_s = _ab_in(
        x, w_in_x, pool_past, pool_w, pool_scale, q_norm_g, kv_norm_g, rope)
    qt_p = _q_up_t(qn, wq_x.T, np.ascontiguousarray(rope[:NP].T), NP)
    k_p, vt_p = _kv_up_t(ckv_p, kpe_p, wk, wv.T)
    q_s = _q_up(qn, wq_x, rope, NP, NS)
    k_s, v_s = _kv_up(ckv_s, kpe_s, wkv_x)
    k_past, v_past = _kv_up(cache_ckv.reshape(DEC_BATCH * PAST_LEN, KV_LORA),
                            cache_kpe.reshape(DEC_BATCH * PAST_LEN, ROPE_DIM), wkv_x)
    att_p = _attn_prompt(qt_p, k_p, vt_p)
    att_s = _attn_sample(q_s, k_past, v_past, k_s, v_s)
    x = _ab_out(x, ypool, att_p, att_s, w_out_ab, ln_g[0, 1], ln_b[0, 1])
    x = _ffn((x,), post_up, post_down, 0, ln_g[0, 2], ln_b[0, 2])

    skip = POOL_EXT - POOL_STATE
    pool_p = utail[:NP // DEC_SEQ].reshape(BATCH, SEQ // DEC_SEQ, POOL_EXT, POOL_DIM)[:, -1, skip:]
    pool_s = utail[NP // DEC_SEQ:, skip:]
    ckv_p = ckv_p.reshape(BATCH, SEQ, KV_LORA)
    ckv_s = ckv_s.reshape(DEC_BATCH, DEC_SEQ, KV_LORA)
    kpe_p = kpe_p.reshape(BATCH, SEQ, ROPE_DIM)
    kpe_s = kpe_s.reshape(DEC_BATCH, DEC_SEQ, ROPE_DIM)

    x = _ffn((x,), pre_up, pre_down, 1, ln_g[1, 0], ln_b[1, 0])
    nzx = SSM_INNER + CONV_DIM
    w_ssd = w_in_ssd.astype(BF16)
    zx, dtx = _ssd_in(x, w_ssd, jnp.pad(w_ssd[:, nzx:], ((0, 0), (0, LANES - SSM_HEADS))))
    conv_w8 = jnp.pad(conv_w, ((0, SUBLANES - CONV_WIDTH), (0, 0)))
    pad_lanes = lambda v, fill: jnp.pad(v.astype(F32), (0, LANES - v.shape[0]), constant_values=fill).reshape(1, LANES)
    a_neg = -jnp.exp(pad_lanes(a_log, 0.0))
    d_skip_x = jnp.repeat(d_skip.astype(F32), SSM_HEAD_DIM).reshape(1, SSM_INNER)
    onehot = (jnp.arange(LANES)[:, None] == (jnp.arange(SSM_INNER) // SSM_HEAD_DIM)[None, :]).astype(BF16)
    common = (conv_w8, conv_b.reshape(1, -1), pad_lanes(dt_bias, 0.0), a_neg, d_skip_x,
              ssm_norm_g.reshape(1, -1), onehot)
    hist = SUBLANES - (CONV_WIDTH - 1)
    y_p, h_p = _ssd_core(zx, dtx, 0, NP // SSD_L, SSD_L, False,
                         jnp.zeros((BATCH, SUBLANES, CONV_DIM), F32),
                         jnp.zeros((BATCH, SSM_STATE, SSM_INNER), F32), *common)
    y_s, h_s = _ssd_core(zx, dtx, NP, DEC_BATCH, DEC_SEQ, True,
                         jnp.pad(state_conv, ((0, 0), (hist, 0), (0, 0))),
                         _state_to_t(state_ssm), *common)
    x = _ssd_out(x, y_p, y_s, w_out_ssd, ln_g[1, 1], ln_b[1, 1])
    y_prompt, y_sample = _ffn((x,), post_up, post_down, 1, ln_g[1, 2], ln_b[1, 2], split_out=True)

    tail = CONV_WIDTH - 1
    conv_p = jnp.stack([zx[(b + 1) * SEQ - tail:(b + 1) * SEQ, SSM_INNER:] for b in range(BATCH)])
    conv_s = zx[NP:].reshape(DEC_BATCH, DEC_SEQ, nzx)[:, DEC_SEQ - tail:, SSM_INNER:]
    ssm_p = _state_from_t(h_p)
    ssm_s = _state_from_t(h_s)

    return (y_prompt.reshape(BATCH, SEQ, D_MODEL), y_sample.reshape(DEC_BATCH, DEC_SEQ, D_MODEL),
            pool_p, pool_s, ckv_p, ckv_s, kpe_p, kpe_s, conv_p, conv_s, ssm_p, ssm_s)
```

```python
import functools

import jax
import jax.numpy as jnp
import numpy as np
from jax import lax
from jax.experimental import pallas as pl
from jax.experimental.pallas import tpu as pltpu

F32 = jnp.float32
BF16 = jnp.bfloat16

D_MODEL = 2048
BATCH = 1
SEQ = 16384
DEPTH = 2
DEC_BATCH = 16
DEC_SEQ = 64
PAST_LEN = 1024
CHUNK = 64
ALPHA = (2 * DEPTH) ** 0.25
LN_EPS = 1e-5
RMS_EPS = 1e-6
FF_DIM = 5504
POOL_WINDOWS = (2, 4, 8, 16)
POOL_DIM = 512
POOL_GROUP = 128
POOL_STATE = 15
NOPE_DIM = 128
ROPE_DIM = 64
V_DIM = 128
MLA_HEADS = 12
Q_LORA = 512
KV_LORA = 512
ROPE_THETA = 10000.0
SSM_INNER = 4096
SSM_HEAD_DIM = 64
SSM_HEADS = 64
SSM_GROUPS = 8
SSM_HPG = 8
SSM_STATE = 128
CONV_WIDTH = 4
CONV_DIM = SSM_INNER + 2 * SSM_GROUPS * SSM_STATE

NP = BATCH * SEQ
NS = DEC_BATCH * DEC_SEQ
NTOK = NP + NS

LANES = 128
SUBLANES = 8
VMEM_LIMIT_BYTES = 56 * 1024 * 1024

FF_TILE = 512
FF_TILES = -(-FF_DIM // FF_TILE)
FF_LAST_OVERLAP = (FF_TILES - 1) * FF_TILE - (FF_DIM - FF_TILE)
FFN_TM = 512
TOK_TM = 512
OUT_ROW_SPLIT = 2
ATT_TQ = 1024
ATT_TK = 512
SSD_L = 128
QK_PAD = 256
V_AUG = V_DIM + 2 * SUBLANES
SSD_IN_TM = 1024
SSD_IN_TN = 1024
POOL_EXT = 16
ATT_SCALE = float((NOPE_DIM + ROPE_DIM) ** -0.5)
ATT_SCALE_LOG2E = ATT_SCALE * 1.4426950408889634


def _cparams(sem):
    return pltpu.CompilerParams(dimension_semantics=sem, vmem_limit_bytes=VMEM_LIMIT_BYTES)


def _layer_norm(y, g, b):
    mu = jnp.mean(y, axis=-1, keepdims=True)
    d = y - mu
    var = jnp.mean(d * d, axis=-1, keepdims=True)
    return d * lax.rsqrt(var + LN_EPS) * g + b


def _rms(h, g):
    return h * lax.rsqrt(jnp.mean(h * h, axis=-1, keepdims=True) + RMS_EPS) * g


def _silu(x):
    h = 0.5 * x
    return h + h * jnp.tanh(h)


def _dot(a, b):
    return jnp.dot(a, b, preferred_element_type=F32)


def _dot_nt(a, b):
    return lax.dot_general(a, b, (((1,), (1,)), ((), ())), preferred_element_type=F32)


def _dot_tn(a, b):
    return lax.dot_general(a, b, (((0,), (0,)), ((), ())), preferred_element_type=F32)


def _split2(v):
    hi = v.astype(BF16)
    lo = (v - hi.astype(F32)).astype(BF16)
    return hi, lo


def _split3(v):
    hi = v.astype(BF16)
    r = v - hi.astype(F32)
    mid = r.astype(BF16)
    lo = (r - mid.astype(F32)).astype(BF16)
    return hi, mid, lo


def _ffn_kernel(npt, n_in, n_out, *refs):
    x_refs = refs[:n_in]
    wg_ref, wu_ref, wd_ref, g_ref, b_ref = refs[n_in:n_in + 5]
    o_refs = refs[n_in + 5:n_in + 5 + n_out]
    xb_ref, h_ref = refs[n_in + 5 + n_out:n_in + 7 + n_out]
    acc_ref = o_refs[0] if n_out == 1 else refs[n_in + 7 + n_out]
    i = pl.program_id(0)
    j = pl.program_id(1)
    nt = FF_TILES

    def load_x():
        if n_in == 1:
            return x_refs[0][...]
        return jnp.where(i < npt, x_refs[0][...], x_refs[1][...])

    def hidden(first_col):
        xb = xb_ref[...]
        h = _silu(_dot(xb, wg_ref[0])) * _dot(xb, wu_ref[0])
        col = lax.broadcasted_iota(jnp.int32, h.shape, 1)
        return jnp.where(col >= first_col, h, 0.0).astype(BF16)

    @pl.when(j == 0)
    def _():
        xb_ref[...] = load_x().astype(BF16)
        acc_ref[...] = jnp.zeros_like(acc_ref)
        h_ref[...] = hidden(0)

    @pl.when(jnp.logical_and(j > 0, j < nt))
    def _():
        part = _dot(h_ref[...], wd_ref[0])
        h_next = hidden(jnp.where(j == nt - 1, FF_LAST_OVERLAP, 0))
        acc_ref[...] += part
        h_ref[...] = h_next

    @pl.when(j == nt)
    def _():
        acc = acc_ref[...] + _dot(h_ref[...], wd_ref[0])
        y = _layer_norm(ALPHA * load_x() + 0.5 * acc, g_ref[...], b_ref[...])
        if n_out == 1:
            o_refs[0][...] = y
        else:
            @pl.when(i < npt)
            def _():
                o_refs[0][...] = y

            @pl.when(i >= npt)
            def _():
                o_refs[1][...] = y


def _ffn(xs, w_up_b, w_down_b, layer, g, b, split_out=False):
    tm = FFN_TM
    npt = NP // tm
    nt = FF_TILES
    el = pl.Element
    aligned = lambda v: pl.multiple_of(v, LANES)
    tile_start = lambda t: aligned(jnp.minimum(t * FF_TILE, FF_DIM - FF_TILE))
    up_start = lambda j: tile_start(jnp.minimum(j, nt - 1))
    down_start = lambda j: tile_start(jnp.maximum(j - 1, 0))
    n_in = len(xs)
    n_out = 2 if split_out else 1
    prompt_tile = lambda i, j: (jnp.minimum(i, npt - 1), 0)
    sample_tile = lambda i, j: (jnp.maximum(i - npt, 0), 0)
    whole_tile = lambda i, j: (i, 0)
    x_specs = [pl.BlockSpec((tm, D_MODEL), m) for m in ((whole_tile,) if n_in == 1 else (prompt_tile, sample_tile))]
    if split_out:
        out_specs = [pl.BlockSpec((tm, D_MODEL), prompt_tile), pl.BlockSpec((tm, D_MODEL), sample_tile)]
        out_shape = [jax.ShapeDtypeStruct((NP, D_MODEL), F32), jax.ShapeDtypeStruct((NS, D_MODEL), F32)]
        scratch = [pltpu.VMEM((tm, D_MODEL), BF16), pltpu.VMEM((tm, FF_TILE), BF16), pltpu.VMEM((tm, D_MODEL), F32)]
    else:
        out_specs = pl.BlockSpec((tm, D_MODEL), whole_tile)
        out_shape = jax.ShapeDtypeStruct((NTOK, D_MODEL), F32)
        scratch = [pltpu.VMEM((tm, D_MODEL), BF16), pltpu.VMEM((tm, FF_TILE), BF16)]
    return pl.pallas_call(
        functools.partial(_ffn_kernel, npt, n_in, n_out),
        grid=(NTOK // tm, nt + 1),
        in_specs=x_specs + [
            pl.BlockSpec((el(1), el(D_MODEL), el(FF_TILE)), lambda i, j: (layer, 0, up_start(j))),
            pl.BlockSpec((el(1), el(D_MODEL), el(FF_TILE)), lambda i, j: (layer, 0, aligned(FF_DIM + up_start(j)))),
            pl.BlockSpec((el(1), el(FF_TILE), el(D_MODEL)), lambda i, j: (layer, down_start(j), 0)),
            pl.BlockSpec((1, D_MODEL), lambda i, j: (0, 0)),
            pl.BlockSpec((1, D_MODEL), lambda i, j: (0, 0)),
        ],
        out_specs=out_specs,
        out_shape=out_shape,
        scratch_shapes=scratch,
        compiler_params=_cparams(("arbitrary", "arbitrary")),
        name="ffn_postnorm",
    )(*xs, w_up_b, w_up_b, w_down_b, g.reshape(1, -1), b.reshape(1, -1))


def _pool_window_mean_minus(ext_ref, base, rows, pos0):
    first = base + POOL_EXT
    t = lax.broadcasted_iota(jnp.int32, (rows, POOL_GROUP), 0)
    posp1 = (pos0 + t + 1).astype(F32)
    outs = []
    for g, w in enumerate(POOL_WINDOWS):
        cols = slice(g * POOL_GROUP, (g + 1) * POOL_GROUP)
        cur = ext_ref[pl.ds(first, rows), cols]
        tot = cur
        for s in range(1, w):
            tot = tot + ext_ref[pl.ds(first - s, rows), cols]
        outs.append(tot / jnp.minimum(posp1, float(w)) - cur)
    return outs


def _ab_in_kernel(n_prompt_tiles, x_ref, w_ref, past_ref, pw_ref, ps_ref, qg_ref, kg_ref, rope_ref,
                  ut_ref, yp_ref, qn_ref, ckvp_ref, ckvs_ref, kpep_ref, kpes_ref, ext_ref):
    i = pl.program_id(0)
    tm = x_ref.shape[0]
    h = _dot(x_ref[...].astype(BF16), w_ref[...])
    u = h[:, :POOL_DIM]
    for s in range(tm // DEC_SEQ):
        ut_ref[s] = u[(s + 1) * DEC_SEQ - POOL_EXT:(s + 1) * DEC_SEQ, :]
    qn_ref[...] = _rms(h[:, POOL_DIM:POOL_DIM + Q_LORA], qg_ref[...]).astype(BF16)
    ckv = _rms(h[:, POOL_DIM + Q_LORA:POOL_DIM + Q_LORA + KV_LORA], kg_ref[...])
    kv = h[:, POOL_DIM + Q_LORA + KV_LORA:] * rope_ref[...]
    kpe = (kv + pltpu.roll(kv, ROPE_DIM, axis=1))[:, :ROPE_DIM]

    @pl.when(i < n_prompt_tiles)
    def _():
        ckvp_ref[...] = ckv
        kpep_ref[...] = kpe

    @pl.when(i >= n_prompt_tiles)
    def _():
        ckvs_ref[...] = ckv
        kpes_ref[...] = kpe

    def finish(d_groups):
        for g in range(len(POOL_WINDOWS)):
            cols = slice(g * POOL_GROUP, (g + 1) * POOL_GROUP)
            y = _dot(d_groups[g].astype(BF16), pw_ref[g]) * ps_ref[:, cols]
            yp_ref[:, cols] = y.astype(BF16)

    @pl.when(i < n_prompt_tiles)
    def _():
        @pl.when(i == 0)
        def _():
            ext_ref[pl.ds(0, POOL_EXT), :] = jnp.zeros((POOL_EXT, POOL_DIM), F32)

        ext_ref[pl.ds(POOL_EXT, tm), :] = u
        finish(_pool_window_mean_minus(ext_ref, 0, tm, i * tm))
        ext_ref[pl.ds(0, POOL_EXT), :] = u[tm - POOL_EXT:, :]

    @pl.when(i >= n_prompt_tiles)
    def _():
        nseg = tm // DEC_SEQ
        stride = POOL_EXT + DEC_SEQ
        parts = [[] for _ in POOL_WINDOWS]
        for s in range(nseg):
            ext_ref[pl.ds(s * stride, POOL_EXT), :] = past_ref[s]
            ext_ref[pl.ds(s * stride + POOL_EXT, DEC_SEQ), :] = u[s * DEC_SEQ:(s + 1) * DEC_SEQ, :]
        for s in range(nseg):
            d = _pool_window_mean_minus(ext_ref, s * stride, DEC_SEQ, PAST_LEN)
            for g in range(len(POOL_WINDOWS)):
                parts[g].append(d[g])
        finish([jnp.concatenate(p, axis=0) for p in parts])


def _ab_in(x, w_in_x, pool_past, pool_w, pool_scale, q_norm_g, kv_norm_g, rope_k):
    n = x.shape[0]
    tm = TOK_TM
    nseg = tm // DEC_SEQ
    npt = NP // tm
    wcols = w_in_x.shape[1]
    ext_rows = max(tm + POOL_EXT, nseg * (POOL_EXT + DEC_SEQ))
    const = lambda i: (0, 0)
    prompt_tile = lambda i: (jnp.minimum(i, npt - 1), 0)
    sample_tile = lambda i: (jnp.maximum(i - npt, 0), 0)
    return pl.pallas_call(
        functools.partial(_ab_in_kernel, npt),
        grid=(n // tm,),
        in_specs=[
            pl.BlockSpec((tm, D_MODEL), lambda i: (i, 0)),
            pl.BlockSpec((D_MODEL, wcols), const),
            pl.BlockSpec((nseg, POOL_EXT, POOL_DIM), lambda i: (jnp.maximum(i - npt, 0), 0, 0)),
            pl.BlockSpec((len(POOL_WINDOWS), POOL_GROUP, POOL_GROUP), lambda i: (0, 0, 0)),
            pl.BlockSpec((1, POOL_DIM), const),
            pl.BlockSpec((1, Q_LORA), const),
            pl.BlockSpec((1, KV_LORA), const),
            pl.BlockSpec((tm, LANES), lambda i: (i, 0)),
        ],
        out_specs=[
            pl.BlockSpec((nseg, POOL_EXT, POOL_DIM), lambda i: (i, 0, 0)),
            pl.BlockSpec((tm, POOL_DIM), lambda i: (i, 0)),
            pl.BlockSpec((tm, Q_LORA), lambda i: (i, 0)),
            pl.BlockSpec((tm, KV_LORA), prompt_tile),
            pl.BlockSpec((tm, KV_LORA), sample_tile),
            pl.BlockSpec((tm, ROPE_DIM), prompt_tile),
            pl.BlockSpec((tm, ROPE_DIM), sample_tile),
        ],
        out_shape=[
            jax.ShapeDtypeStruct((n // DEC_SEQ, POOL_EXT, POOL_DIM), F32),
            jax.ShapeDtypeStruct((n, POOL_DIM), BF16),
            jax.ShapeDtypeStruct((n, Q_LORA), BF16),
            jax.ShapeDtypeStruct((NP, KV_LORA), F32),
            jax.ShapeDtypeStruct((NS, KV_LORA), F32),
            jax.ShapeDtypeStruct((NP, ROPE_DIM), F32),
            jax.ShapeDtypeStruct((NS, ROPE_DIM), F32),
        ],
        scratch_shapes=[pltpu.VMEM((ext_rows, POOL_DIM), F32)],
        compiler_params=_cparams(("arbitrary",)),
        name="ab_in",
    )(x, w_in_x, pool_past, pool_w.astype(BF16), pool_scale.reshape(1, -1),
      q_norm_g.reshape(1, -1), kv_norm_g.reshape(1, -1), rope_k)


def _q_up_kernel(qn_ref, w_ref, rope_ref, q_ref):
    qn = qn_ref[...]
    rope = rope_ref[...]
    lane = lax.broadcasted_iota(jnp.int32, rope.shape, 1)
    hpc = 4
    ncol = hpc * LANES
    for c in range(MLA_HEADS // hpc):
        nope = _dot(qn, w_ref[:, c * ncol:(c + 1) * ncol])
        rp = _dot(qn, w_ref[:, MLA_HEADS * LANES + c * ncol:MLA_HEADS * LANES + (c + 1) * ncol])
        for hh in range(hpc):
            hd = c * hpc + hh
            v = rp[:, hh * LANES:(hh + 1) * LANES] * rope
            rot = jnp.where(lane < ROPE_DIM, v + pltpu.roll(v, ROPE_DIM, axis=1), 0.0)
            q_ref[hd, :, 0:LANES] = (nope[:, hh * LANES:(hh + 1) * LANES] * ATT_SCALE).astype(BF16)
            q_ref[hd, :, LANES:QK_PAD] = (rot * ATT_SCALE).astype(BF16)


def _q_up(qn, wq_x, rope_q, row0, nrows):
    tm = TOK_TM
    t0 = row0 // tm
    return pl.pallas_call(
        _q_up_kernel,
        grid=(nrows // tm,),
        in_specs=[
            pl.BlockSpec((tm, Q_LORA), lambda i: (t0 + i, 0)),
            pl.BlockSpec(wq_x.shape, lambda i: (0, 0)),
            pl.BlockSpec((tm, LANES), lambda i: (t0 + i, 0)),
        ],
        out_specs=pl.BlockSpec((MLA_HEADS, tm, QK_PAD), lambda i: (0, i, 0)),
        out_shape=jax.ShapeDtypeStruct((MLA_HEADS, nrows, QK_PAD), BF16),
        compiler_params=_cparams(("parallel",)),
        name="q_up",
    )(qn, wq_x, rope_q)


def _q_up_t_kernel(qn_ref, wt_ref, ropet_ref, qt_ref):
    qt_all = _dot_nt(wt_ref[...], qn_ref[...])
    ropet = ropet_ref[...]
    nrope = MLA_HEADS * LANES
    zeros = jnp.zeros((QK_PAD - LANES - ROPE_DIM, qt_all.shape[1]), BF16)
    for hd in range(MLA_HEADS):
        v = qt_all[nrope + hd * LANES:nrope + (hd + 1) * LANES, :] * ropet
        rot = v[:ROPE_DIM] + v[ROPE_DIM:]
        qt_ref[hd, 0, 0:LANES, :] = (qt_all[hd * LANES:(hd + 1) * LANES, :] * ATT_SCALE_LOG2E).astype(BF16)
        qt_ref[hd, 0, LANES:LANES + ROPE_DIM, :] = (rot * ATT_SCALE_LOG2E).astype(BF16)
        qt_ref[hd, 0, LANES + ROPE_DIM:QK_PAD, :] = zeros


def _q_up_t(qn, wq_xt, rope_t, nrows):
    tm = ATT_TQ
    return pl.pallas_call(
        _q_up_t_kernel,
        grid=(nrows // tm,),
        in_specs=[
            pl.BlockSpec((tm, Q_LORA), lambda i: (i, 0)),
            pl.BlockSpec(wq_xt.shape, lambda i: (0, 0)),
            pl.BlockSpec((LANES, tm), lambda i: (0, i)),
        ],
        out_specs=pl.BlockSpec((MLA_HEADS, 1, QK_PAD, tm), lambda i: (0, i, 0, 0)),
        out_shape=jax.ShapeDtypeStruct((MLA_HEADS, nrows // tm, QK_PAD, tm), BF16),
        compiler_params=_cparams(("parallel",)),
        name="q_up_t",
    )(qn, wq_xt, rope_t)


def _kv_up_kernel(ckv_ref, kpe_ref, w_ref, k_ref, v_ref):
    c = ckv_ref[...].astype(BF16)
    kpe = kpe_ref[...].astype(BF16)
    hpc = 4
    ncol = hpc * LANES
    for j in range(MLA_HEADS // hpc):
        kn = _dot(c, w_ref[:, j * ncol:(j + 1) * ncol])
        vv = _dot(c, w_ref[:, MLA_HEADS * LANES + j * ncol:MLA_HEADS * LANES + (j + 1) * ncol])
        for hh in range(hpc):
            hd = j * hpc + hh
            k_ref[hd, :, 0:LANES] = kn[:, hh * LANES:(hh + 1) * LANES].astype(BF16)
            k_ref[hd, :, LANES:LANES + ROPE_DIM] = kpe
            k_ref[hd, :, LANES + ROPE_DIM:QK_PAD] = jnp.zeros_like(kpe)
            v_ref[hd] = vv[:, hh * LANES:(hh + 1) * LANES].astype(BF16)


def _kv_up(ckv, kpe, wkv_x):
    tm = TOK_TM
    nrows = ckv.shape[0]
    return pl.pallas_call(
        _kv_up_kernel,
        grid=(nrows // tm,),
        in_specs=[
            pl.BlockSpec((tm, KV_LORA), lambda i: (i, 0)),
            pl.BlockSpec((tm, ROPE_DIM), lambda i: (i, 0)),
            pl.BlockSpec(wkv_x.shape, lambda i: (0, 0)),
        ],
        out_specs=[
            pl.BlockSpec((MLA_HEADS, tm, QK_PAD), lambda i: (0, i, 0)),
            pl.BlockSpec((MLA_HEADS, tm, V_DIM), lambda i: (0, i, 0)),
        ],
        out_shape=[
            jax.ShapeDtypeStruct((MLA_HEADS, nrows, QK_PAD), BF16),
            jax.ShapeDtypeStruct((MLA_HEADS, nrows, V_DIM), BF16),
        ],
        compiler_params=_cparams(("parallel",)),
        name="kv_up",
    )(ckv, kpe, wkv_x)


def _kv_up_t_kernel(ckv_ref, kpe_ref, wk_ref, wvt_ref, k_ref, vt_ref):
    c = ckv_ref[...].astype(BF16)
    kpe = kpe_ref[...].astype(BF16)
    hpc = 4
    ncol = hpc * LANES
    for j in range(MLA_HEADS // hpc):
        kn = _dot(c, wk_ref[:, j * ncol:(j + 1) * ncol])
        for hh in range(hpc):
            hd = j * hpc + hh
            k_ref[hd, :, 0:LANES] = kn[:, hh * LANES:(hh + 1) * LANES].astype(BF16)
            k_ref[hd, :, LANES:LANES + ROPE_DIM] = kpe
            k_ref[hd, :, LANES + ROPE_DIM:QK_PAD] = jnp.zeros_like(kpe)
    vt_all = _dot_nt(wvt_ref[...], c)
    ones = jnp.ones((V_AUG - V_DIM, vt_all.shape[1]), BF16)
    for hd in range(MLA_HEADS):
        vt_ref[hd, 0, 0:V_DIM, :] = vt_all[hd * V_DIM:(hd + 1) * V_DIM, :].astype(BF16)
        vt_ref[hd, 0, V_DIM:V_AUG, :] = ones


def _kv_up_t(ckv, kpe, wk, wvt):
    tm = ATT_TK
    nrows = ckv.shape[0]
    return pl.pallas_call(
        _kv_up_t_kernel,
        grid=(nrows // tm,),
        in_specs=[
            pl.BlockSpec((tm, KV_LORA), lambda i: (i, 0)),
            pl.BlockSpec((tm, ROPE_DIM), lambda i: (i, 0)),
            pl.BlockSpec(wk.shape, lambda i: (0, 0)),
            pl.BlockSpec(wvt.shape, lambda i: (0, 0)),
        ],
        out_specs=[
            pl.BlockSpec((MLA_HEADS, tm, QK_PAD), lambda i: (0, i, 0)),
            pl.BlockSpec((MLA_HEADS, 1, V_AUG, tm), lambda i: (0, i, 0, 0)),
        ],
        out_shape=[
            jax.ShapeDtypeStruct((MLA_HEADS, nrows, QK_PAD), BF16),
            jax.ShapeDtypeStruct((MLA_HEADS, nrows // tm, V_AUG, tm), BF16),
        ],
        compiler_params=_cparams(("parallel",)),
        name="kv_up_t",
    )(ckv, kpe, wk, wvt)


def _attn_prompt_kernel(qt_ref, k_ref, vt_ref, o_ref, s0_ref, s1_ref, m_ref, acc_ref):
    i = pl.program_id(1)
    tk = s0_ref.shape[0]
    tq = qt_ref.shape[3]

    def scores(c, s_ref):
        start = pl.multiple_of(c * tk, tk)
        s_ref[:, 0:tq] = _dot(k_ref[0, pl.ds(start, tk), :], qt_ref[0, 0])

    def softmax_pv(c, s_ref, mask):
        st = s_ref[:, 0:tq]
        if mask is not None:
            st = jnp.where(mask, st, -jnp.inf)
        m = m_ref[...]
        m_new = jnp.maximum(m, jnp.max(st, axis=0, keepdims=True))
        alpha = jnp.exp2(m - m_new)
        p = jnp.exp2(st - m_new)
        m_ref[...] = m_new
        acc_ref[...] = alpha * acc_ref[...] + _dot(vt_ref[0, c], p.astype(BF16))

    m_ref[...] = jnp.full(m_ref.shape, -jnp.inf, F32)
    acc_ref[...] = jnp.zeros(acc_ref.shape, F32)
    key_chunk = lax.broadcasted_iota(jnp.int32, (tk, tq), 0) // CHUNK
    query_chunk = lax.broadcasted_iota(jnp.int32, (tk, tq), 1) // CHUNK

    scores(0, s0_ref)

    def pair(j, carry):
        scores(2 * j + 1, s1_ref)
        softmax_pv(2 * j, s0_ref, None)
        scores(2 * j + 2, s0_ref)
        softmax_pv(2 * j + 1, s1_ref, None)
        return carry

    lax.fori_loop(0, i, pair, 0)
    scores(2 * i + 1, s1_ref)
    softmax_pv(2 * i, s0_ref, query_chunk >= key_chunk)
    softmax_pv(2 * i + 1, s1_ref, query_chunk >= key_chunk + tk // CHUNK)

    o_ref[...] = (acc_ref[0:V_DIM, :] / acc_ref[V_DIM:V_DIM + 1, :]).T.astype(BF16)


def _attn_prompt(qt, k, vt):
    tq, tk = ATT_TQ, ATT_TK
    assert tq == 2 * tk and tk % CHUNK == 0
    score = pltpu.VMEM((tk, tq + LANES), F32)
    return pl.pallas_call(
        _attn_prompt_kernel,
        grid=(MLA_HEADS, NP // tq),
        in_specs=[
            pl.BlockSpec((1, 1, QK_PAD, tq), lambda h, i: (h, i, 0, 0)),
            pl.BlockSpec((1, NP, QK_PAD), lambda h, i: (h, 0, 0)),
            pl.BlockSpec((1, NP // tk, V_AUG, tk), lambda h, i: (h, 0, 0, 0)),
        ],
        out_specs=pl.BlockSpec((tq, V_DIM), lambda h, i: (i, h)),
        out_shape=jax.ShapeDtypeStruct((NP, MLA_HEADS * V_DIM), BF16),
        scratch_shapes=[score, score, pltpu.VMEM((1, tq), F32), pltpu.VMEM((V_AUG, tq), F32)],
        compiler_params=_cparams(("parallel", "arbitrary")),
        name="attn_prompt",
    )(qt, k, vt)


def _attn_sample_kernel(q_ref, kp_ref, vp_ref, kn_ref, vn_ref, o_ref):
    for hd in range(MLA_HEADS):
        q = q_ref[hd]
        sp = _dot_nt(q, kp_ref[hd])
        sn = _dot_nt(q, kn_ref[hd])
        m = jnp.maximum(jnp.max(sp, axis=1, keepdims=True), jnp.max(sn, axis=1, keepdims=True))
        pp = jnp.exp(sp - m)
        pn = jnp.exp(sn - m)
        l = jnp.sum(pp, axis=1, keepdims=True) + jnp.sum(pn, axis=1, keepdims=True)
        o = _dot(pp.astype(BF16), vp_ref[hd]) + _dot(pn.astype(BF16), vn_ref[hd])
        o_ref[:, hd * V_DIM:(hd + 1) * V_DIM] = (o / l).astype(BF16)


def _attn_sample(q, k_past, v_past, k_new, v_new):
    assert PAST_LEN % CHUNK == 0 and DEC_SEQ <= CHUNK
    return pl.pallas_call(
        _attn_sample_kernel,
        grid=(DEC_BATCH,),
        in_specs=[
            pl.BlockSpec((MLA_HEADS, DEC_SEQ, QK_PAD), lambda b: (0, b, 0)),
            pl.BlockSpec((MLA_HEADS, PAST_LEN, QK_PAD), lambda b: (0, b, 0)),
            pl.BlockSpec((MLA_HEADS, PAST_LEN, V_DIM), lambda b: (0, b, 0)),
            pl.BlockSpec((MLA_HEADS, DEC_SEQ, QK_PAD), lambda b: (0, b, 0)),
            pl.BlockSpec((MLA_HEADS, DEC_SEQ, V_DIM), lambda b: (0, b, 0)),
        ],
        out_specs=pl.BlockSpec((DEC_SEQ, MLA_HEADS * V_DIM), lambda b: (b, 0)),
        out_shape=jax.ShapeDtypeStruct((NS, MLA_HEADS * V_DIM), BF16),
        compiler_params=_cparams(("parallel",)),
        name="attn_sample",
    )(q, k_past, v_past, k_new, v_new)


def _ab_out_kernel(npt, x_ref, yp_ref, ap_ref, as_ref, wp_ref, wa_ref, g_ref, b_ref, o_ref):
    i = pl.program_id(0)
    half = x_ref.shape[0] // OUT_ROW_SPLIT
    for r in range(OUT_ROW_SPLIT):
        rows = pl.ds(r * half, half)
        att = jnp.where(i < npt, ap_ref[rows, :], as_ref[rows, :])
        mix = _dot(yp_ref[rows, :], wp_ref[...]) + _dot(att, wa_ref[...])
        o_ref[rows, :] = _layer_norm(ALPHA * x_ref[rows, :] + mix, g_ref[...], b_ref[...])


def _ab_out(x, ypool, att_p, att_s, w_out, g, b):
    n = x.shape[0]
    tm = TOK_TM
    npt = NP // tm
    w = w_out.astype(BF16)
    adim = MLA_HEADS * V_DIM
    const = lambda i: (0, 0)
    return pl.pallas_call(
        functools.partial(_ab_out_kernel, npt),
        grid=(n // tm,),
        in_specs=[
            pl.BlockSpec((tm, D_MODEL), lambda i: (i, 0)),
            pl.BlockSpec((tm, POOL_DIM), lambda i: (i, 0)),
            pl.BlockSpec((tm, adim), lambda i: (jnp.minimum(i, npt - 1), 0)),
            pl.BlockSpec((tm, adim), lambda i: (jnp.maximum(i - npt, 0), 0)),
            pl.BlockSpec((POOL_DIM, D_MODEL), const),
            pl.BlockSpec((adim, D_MODEL), const),
            pl.BlockSpec((1, D_MODEL), const),
            pl.BlockSpec((1, D_MODEL), const),
        ],
        out_specs=pl.BlockSpec((tm, D_MODEL), lambda i: (i, 0)),
        out_shape=jax.ShapeDtypeStruct((n, D_MODEL), F32),
        compiler_params=_cparams(("parallel",)),
        name="ab_out",
    )(x, ypool, att_p, att_s, w[:POOL_DIM], w[POOL_DIM:], g.reshape(1, -1), b.reshape(1, -1))


def _ssd_out_kernel(npt, x_ref, ap_ref, as_ref, w_ref, g_ref, b_ref, o_ref):
    i = pl.program_id(0)
    half = x_ref.shape[0] // OUT_ROW_SPLIT
    for r in range(OUT_ROW_SPLIT):
        rows = pl.ds(r * half, half)
        act = jnp.where(i < npt, ap_ref[rows, :], as_ref[rows, :])
        o_ref[rows, :] = _layer_norm(ALPHA * x_ref[rows, :] + _dot(act, w_ref[...]), g_ref[...], b_ref[...])


def _ssd_out(x, act_p, act_s, w_out, g, b):
    n = x.shape[0]
    tm = TOK_TM
    npt = NP // tm
    const = lambda i: (0, 0)
    return pl.pallas_call(
        functools.partial(_ssd_out_kernel, npt),
        grid=(n // tm,),
        in_specs=[
            pl.BlockSpec((tm, D_MODEL), lambda i: (i, 0)),
            pl.BlockSpec((tm, SSM_INNER), lambda i: (jnp.minimum(i, npt - 1), 0)),
            pl.BlockSpec((tm, SSM_INNER), lambda i: (jnp.maximum(i - npt, 0), 0)),
            pl.BlockSpec((SSM_INNER, D_MODEL), const),
            pl.BlockSpec((1, D_MODEL), const),
            pl.BlockSpec((1, D_MODEL), const),
        ],
        out_specs=pl.BlockSpec((tm, D_MODEL), lambda i: (i, 0)),
        out_shape=jax.ShapeDtypeStruct((n, D_MODEL), F32),
        compiler_params=_cparams(("parallel",)),
        name="ssd_out",
    )(x, act_p, act_s, w_out.astype(BF16), g.reshape(1, -1), b.reshape(1, -1))


def _ssd_in_kernel(x_ref, w_ref, wdt_ref, o_ref, dt_ref, xb_ref):
    @pl.when(pl.program_id(1) == 0)
    def _():
        xb_ref[...] = x_ref[...].astype(BF16)
        dt_ref[...] = _dot(xb_ref[...], wdt_ref[...])

    o_ref[...] = _dot(xb_ref[...], w_ref[...])


def _ssd_in(x, w_all, w_dt):
    n = x.shape[0]
    tm = SSD_IN_TM
    ncol = SSM_INNER + CONV_DIM
    w_zx = w_all
    return pl.pallas_call(
        _ssd_in_kernel,
        grid=(n // tm, ncol // SSD_IN_TN),
        in_specs=[
            pl.BlockSpec((tm, D_MODEL), lambda i, j: (i, 0)),
            pl.BlockSpec((D_MODEL, SSD_IN_TN), lambda i, j: (0, j)),
            pl.BlockSpec((D_MODEL, LANES), lambda i, j: (0, 0)),
        ],
        out_specs=[
            pl.BlockSpec((tm, SSD_IN_TN), lambda i, j: (i, j)),
            pl.BlockSpec((tm, LANES), lambda i, j: (i, 0)),
        ],
        out_shape=[
            jax.ShapeDtypeStruct((n, ncol), F32),
            jax.ShapeDtypeStruct((n, LANES), F32),
        ],
        scratch_shapes=[pltpu.VMEM((tm, D_MODEL), BF16)],
        compiler_params=_cparams(("parallel", "arbitrary")),
        name="ssd_in",
    )(x, w_zx, w_dt)


def _ssd_kernel(rows, fresh_each_step, z_ref, xs_ref, bc_ref, dt_ref, cpast_ref, h0_ref, cw_ref,
                cb_ref, dtb_ref, a_ref, dsk_ref, ng_ref, e_ref, y_ref, h_ref, ext_ref):
    c = pl.program_id(0)
    L = SSD_L
    G, R, P, S = SSM_GROUPS, SSM_HPG, SSM_HEAD_DIM, SSM_STATE
    halo = SUBLANES

    def start_sequence():
        ext_ref[pl.ds(0, halo), :] = cpast_ref[0]
        h_ref[0] = h0_ref[0]

    if fresh_each_step:
        start_sequence()
    else:
        pl.when(c == 0)(start_sequence)

    ext_ref[pl.ds(halo, rows), 0:SSM_INNER] = xs_ref[...]
    ext_ref[pl.ds(halo, rows), SSM_INNER:CONV_DIM] = bc_ref[...]
    if rows < L:
        ext_ref[pl.ds(halo + rows, L - rows), :] = jnp.zeros((L - rows, CONV_DIM), F32)

    win = ext_ref[pl.ds(0, halo + L), :]
    conv = cb_ref[...] + cw_ref[CONV_WIDTH - 1:CONV_WIDTH, :] * win[halo:]
    for back in range(1, CONV_WIDTH):
        k = CONV_WIDTH - 1 - back
        conv = conv + cw_ref[k:k + 1, :] * pltpu.roll(win, back, axis=0)[halo:]
    ext_ref[pl.ds(0, halo), :] = ext_ref[pl.ds(rows, halo), :]
    act = _silu(conv)
    xs = act[:, :SSM_INNER]
    xsb = xs.astype(BF16)
    bmat = act[:, SSM_INNER:SSM_INNER + G * S].astype(BF16)
    cmat = act[:, SSM_INNER + G * S:].astype(BF16)

    dtr = dt_ref[...] + dtb_ref[...]
    dt = jnp.maximum(dtr, 0.0) + jnp.log1p(jnp.exp(-jnp.abs(dtr)))
    if rows < L:
        dt = jnp.concatenate([dt, jnp.zeros((L - rows, LANES), F32)], axis=0)
    da = dt * a_ref[...]
    ti = lax.broadcasted_iota(jnp.int32, (L, L), 0)
    si = lax.broadcasted_iota(jnp.int32, (L, L), 1)
    causal = ti >= si
    tri = jnp.where(causal, 1.0, 0.0).astype(BF16)
    acs = sum(_dot(tri, part) for part in _split3(da))
    acs_t = acs.T
    dt_t = dt.T
    last = acs[L - 1:L, :]
    to_end = jnp.exp(last - acs) * dt
    eacs = jnp.exp(acs)
    onehot = e_ref[...]
    expand = lambda v: sum(_dot(part, onehot) for part in _split2(v))
    x_end = (xs * expand(to_end)).astype(BF16)
    eacs_x = expand(eacs)
    dec_x = sum(_dot(part, onehot) for part in _split3(jnp.broadcast_to(jnp.exp(last), (SUBLANES, LANES))))[0:1, :]

    lane4 = lax.broadcasted_iota(jnp.int32, (L, 4 * P), 1) // P
    for g in range(G):
        bg = bmat[:, g * S:(g + 1) * S]
        cg = cmat[:, g * S:(g + 1) * S]
        cbm = _dot_nt(cg, bg)
        cols = slice(g * R * P, (g + 1) * R * P)
        ht = h_ref[0, :, cols]
        y_state = _dot(cg, ht.astype(BF16)) * eacs_x[:, cols]
        y_parts = []
        for half in range(R // 4):
            x4 = xsb[:, g * R * P + half * 4 * P:g * R * P + (half + 1) * 4 * P]
            x_bd = jnp.concatenate([jnp.where(lane4 == r, x4, jnp.zeros_like(x4)) for r in range(4)], axis=0)
            wts = []
            for r in range(4):
                hd = g * R + half * 4 + r
                seg = acs[:, hd:hd + 1] - acs_t[hd:hd + 1, :]
                decay = jnp.exp(jnp.where(causal, seg, -jnp.inf))
                wts.append((cbm * decay * dt_t[hd:hd + 1, :]).astype(BF16))
            y_parts.append(_dot(jnp.concatenate(wts, axis=1), x_bd))
        y = jnp.concatenate(y_parts, axis=1) + y_state + dsk_ref[:, cols] * xs[:, cols]
        gz = y[:rows] * _silu(z_ref[:, cols])
        gz = gz * lax.rsqrt(jnp.mean(gz * gz, axis=-1, keepdims=True) + RMS_EPS)
        y_ref[:, cols] = (gz * ng_ref[:, cols]).astype(BF16)
        h_ref[0, :, cols] = ht * dec_x[:, cols] + _dot_tn(bg, x_end[:, cols])


def _ssd_core(zx, dtx, row0, nblk, rows, fresh_each_step, conv_past8, h0_t, conv_w8, conv_b, dt_bias, a_neg,
              d_skip_x, norm_g, onehot):
    L = SSD_L
    r0 = row0 // rows
    nseq = nblk if fresh_each_step else 1
    seq = (lambda c: c) if fresh_each_step else (lambda c: 0)
    const = lambda c: (0, 0)
    return pl.pallas_call(
        functools.partial(_ssd_kernel, rows, fresh_each_step),
        grid=(nblk,),
        in_specs=[
            pl.BlockSpec((rows, SSM_INNER), lambda c: (r0 + c, 0)),
            pl.BlockSpec((rows, SSM_INNER), lambda c: (r0 + c, 1)),
            pl.BlockSpec((rows, 2 * SSM_GROUPS * SSM_STATE), lambda c: (r0 + c, 4)),
            pl.BlockSpec((rows, LANES), lambda c: (r0 + c, 0)),
            pl.BlockSpec((1, SUBLANES, CONV_DIM), lambda c: (seq(c), 0, 0)),
            pl.BlockSpec((1, SSM_STATE, SSM_INNER), lambda c: (seq(c), 0, 0)),
            pl.BlockSpec((SUBLANES, CONV_DIM), const),
            pl.BlockSpec((1, CONV_DIM), const),
            pl.BlockSpec((1, LANES), const),
            pl.BlockSpec((1, LANES), const),
            pl.BlockSpec((1, SSM_INNER), const),
            pl.BlockSpec((1, SSM_INNER), const),
            pl.BlockSpec((LANES, SSM_INNER), const),
        ],
        out_specs=[
            pl.BlockSpec((rows, SSM_INNER), lambda c: (c, 0)),
            pl.BlockSpec((1, SSM_STATE, SSM_INNER), lambda c: (seq(c), 0, 0)),
        ],
        out_shape=[
            jax.ShapeDtypeStruct((nblk * rows, SSM_INNER), BF16),
            jax.ShapeDtypeStruct((nseq, SSM_STATE, SSM_INNER), F32),
        ],
        scratch_shapes=[pltpu.VMEM((L + 2 * SUBLANES, CONV_DIM), F32)],
        compiler_params=_cparams(("arbitrary",)),
        name="ssd_core_seq" if not fresh_each_step else "ssd_core_blocks",
    )(zx, zx, zx, dtx, conv_past8, h0_t, conv_w8, conv_b, dt_bias, a_neg, d_skip_x, norm_g, onehot)


def _rope_table():
    half = ROPE_DIM // 2
    inv = (1.0 / (np.float32(ROPE_THETA) ** (np.arange(half, dtype=np.float32) * np.float32(2.0) / np.float32(ROPE_DIM)))).astype(np.float32)
    pos = np.concatenate([
        np.tile(np.arange(SEQ, dtype=np.float32), BATCH),
        np.tile(np.arange(DEC_SEQ, dtype=np.float32) + np.float32(PAST_LEN), DEC_BATCH)])
    ang = (pos[:, None] * inv[None, :]).astype(np.float32).astype(np.float64)
    cos, sin = np.cos(ang), np.sin(ang)
    return np.concatenate([cos, cos, -sin, sin], axis=1).astype(np.float32)


def _swap_halves(w):
    half = w.shape[-1] // 2
    return jnp.concatenate([w[..., half:], w[..., :half]], axis=-1)


def _prep_ab_weights(w_in_ab, w_q_up, w_kv_up):
    kpe_w = w_in_ab[:, POOL_DIM + Q_LORA + KV_LORA:]
    w_in_x = jnp.concatenate([w_in_ab, _swap_halves(kpe_w)], axis=1).astype(BF16)
    wq = w_q_up.reshape(Q_LORA, MLA_HEADS, NOPE_DIM + ROPE_DIM)
    wq_rope = wq[..., NOPE_DIM:]
    wq_x = jnp.concatenate([
        wq[..., :NOPE_DIM].reshape(Q_LORA, -1),
        jnp.concatenate([wq_rope, _swap_halves(wq_rope)], axis=-1).reshape(Q_LORA, -1)], axis=1).astype(BF16)
    wkv = w_kv_up.reshape(KV_LORA, MLA_HEADS, NOPE_DIM + V_DIM)
    wk = wkv[..., :NOPE_DIM].reshape(KV_LORA, -1).astype(BF16)
    wv = wkv[..., NOPE_DIM:].reshape(KV_LORA, -1).astype(BF16)
    return w_in_x, wq_x, wk, wv


def _state_to_t(h):
    b = h.shape[0]
    return jnp.transpose(h, (0, 3, 1, 2)).reshape(b, SSM_STATE, SSM_INNER)


def _state_from_t(ht):
    b = ht.shape[0]
    return jnp.transpose(ht.reshape(b, SSM_STATE, SSM_HEADS, SSM_HEAD_DIM), (0, 2, 3, 1))


def kernel(x_prompt, x_sample, state_pool, cache_ckv, cache_kpe, state_conv, state_ssm, ffn_pre_up, ffn_pre_down, ffn_post_up, ffn_post_down, ln_g, ln_b, w_in_ab, pool_w, pool_scale, q_norm_g, w_q_up, kv_norm_g, w_kv_up, w_out_ab, w_in_ssd, conv_w, conv_b, dt_bias, a_log, d_skip, ssm_norm_g, w_out_ssd):
    pre_up, pre_down = ffn_pre_up.astype(BF16), ffn_pre_down.astype(BF16)
    post_up, post_down = ffn_post_up.astype(BF16), ffn_post_down.astype(BF16)

    x = _ffn((x_prompt.reshape(NP, D_MODEL), x_sample.reshape(NS, D_MODEL)), pre_up, pre_down, 0,
             ln_g[0, 0], ln_b[0, 0])
    rope = _rope_table()
    w_in_x, wq_x, wk, wv = _prep_ab_weights(w_in_ab, w_q_up, w_kv_up)
    wkv_x = jnp.concatenate([wk, wv], axis=1)
    pool_past = jnp.pad(state_pool, ((0, 0), (POOL_EXT - POOL_STATE, 0), (0, 0)))
    utail, ypool, qn, ckv_p, ckv_s, kpe_p, kpe_s = _ab_in(
        x, w_in_x, pool_past, pool_w, pool_scale, q_norm_g, kv_norm_g, rope)
    qt_p = _q_up_t(qn, wq_x.T, np.ascontiguousarray(rope[:NP].T), NP)
    k_p, vt_p = _kv_up_t(ckv_p, kpe_p, wk, wv.T)
    q_s = _q_up(qn, wq_x, rope, NP, NS)
    k_s, v_s = _kv_up(ckv_s, kpe_s, wkv_x)
    k_past, v_past = _kv_up(cache_ckv.reshape(DEC_BATCH * PAST_LEN, KV_LORA),
                            cache_kpe.reshape(DEC_BATCH * PAST_LEN, ROPE_DIM), wkv_x)
    att_p = _attn_prompt(qt_p, k_p, vt_p)
    att_s = _attn_sample(q_s, k_past, v_past, k_s, v_s)
    x = _ab_out(x, ypool, att_p, att_s, w_out_ab, ln_g[0, 1], ln_b[0, 1])
    x = _ffn((x,), post_up, post_down, 0, ln_g[0, 2], ln_b[0, 2])

    skip = POOL_EXT - POOL_STATE
    pool_p = utail[:NP // DEC_SEQ].reshape(BATCH, SEQ // DEC_SEQ, POOL_EXT, POOL_DIM)[:, -1, skip:]
    pool_s = utail[NP // DEC_SEQ:, skip:]
    ckv_p = ckv_p.reshape(BATCH, SEQ, KV_LORA)
    ckv_s = ckv_s.reshape(DEC_BATCH, DEC_SEQ, KV_LORA)
    kpe_p = kpe_p.reshape(BATCH, SEQ, ROPE_DIM)
    kpe_s = kpe_s.reshape(DEC_BATCH, DEC_SEQ, ROPE_DIM)

    x = _ffn((x,), pre_up, pre_down, 1, ln_g[1, 0], ln_b[1, 0])
    nzx = SSM_INNER + CONV_DIM
    w_ssd = w_in_ssd.astype(BF16)
    zx, dtx = _ssd_in(x, w_ssd, jnp.pad(w_ssd[:, nzx:], ((0, 0), (0, LANES - SSM_HEADS))))
    conv_w8 = jnp.pad(conv_w, ((0, SUBLANES - CONV_WIDTH), (0, 0)))
    pad_lanes = lambda v, fill: jnp.pad(v.astype(F32), (0, LANES - v.shape[0]), constant_values=fill).reshape(1, LANES)
    a_neg = -jnp.exp(pad_lanes(a_log, 0.0))
    d_skip_x = jnp.repeat(d_skip.astype(F32), SSM_HEAD_DIM).reshape(1, SSM_INNER)
    onehot = (jnp.arange(LANES)[:, None] == (jnp.arange(SSM_INNER) // SSM_HEAD_DIM)[None, :]).astype(BF16)
    common = (conv_w8, conv_b.reshape(1, -1), pad_lanes(dt_bias, 0.0), a_neg, d_skip_x,
              ssm_norm_g.reshape(1, -1), onehot)
    hist = SUBLANES - (CONV_WIDTH - 1)
    y_p, h_p = _ssd_core(zx, dtx, 0, NP // SSD_L, SSD_L, False,
                         jnp.zeros((BATCH, SUBLANES, CONV_DIM), F32),
                         jnp.zeros((BATCH, SSM_STATE, SSM_INNER), F32), *common)
    y_s, h_s = _ssd_core(zx, dtx, NP, DEC_BATCH, DEC_SEQ, True,
                         jnp.pad(state_conv, ((0, 0), (hist, 0), (0, 0))),
                         _state_to_t(state_ssm), *common)
    x = _ssd_out(x, y_p, y_s, w_out_ssd, ln_g[1, 1], ln_b[1, 1])
    y_prompt, y_sample = _ffn((x,), post_up, post_down, 1, ln_g[1, 2], ln_b[1, 2], split_out=True)

    tail = CONV_WIDTH - 1
    conv_p = jnp.stack([zx[(b + 1) * SEQ - tail:(b + 1) * SEQ, SSM_INNER:] for b in range(BATCH)])
    conv_s = jnp.stack([zx[NP + (b + 1) * DEC_SEQ - tail:NP + (b + 1) * DEC_SEQ, SSM_INNER:]
                        for b in range(DEC_BATCH)])
    ssm_p = _state_from_t(h_p)
    ssm_s = _state_from_t(h_s)

    return (y_prompt.reshape(BATCH, SEQ, D_MODEL), y_sample.reshape(DEC_BATCH, DEC_SEQ, D_MODEL),
            pool_p, pool_s, ckv_p, ckv_s, kpe_p, kpe_s, conv_p, conv_s, ssm_p, ssm_s)
```

```python
import functools

import jax
import jax.numpy as jnp
import numpy as np
from jax import lax
from jax.experimental import pallas as pl
from jax.experimental.pallas import tpu as pltpu

F32 = jnp.float32
BF16 = jnp.bfloat16

D_MODEL = 2048
BATCH = 1
SEQ = 16384
DEPTH = 2
DEC_BATCH = 16
DEC_SEQ = 64
PAST_LEN = 1024
CHUNK = 64
ALPHA = (2 * DEPTH) ** 0.25
LN_EPS = 1e-5
RMS_EPS = 1e-6
FF_DIM = 5504
POOL_WINDOWS = (2, 4, 8, 16)
POOL_DIM = 512
POOL_GROUP = 128
POOL_STATE = 15
NOPE_DIM = 128
ROPE_DIM = 64
V_DIM = 128
MLA_HEADS = 12
Q_LORA = 512
KV_LORA = 512
ROPE_THETA = 10000.0
SSM_INNER = 4096
SSM_HEAD_DIM = 64
SSM_HEADS = 64
SSM_GROUPS = 8
SSM_HPG = 8
SSM_STATE = 128
CONV_WIDTH = 4
CONV_DIM = SSM_INNER + 2 * SSM_GROUPS * SSM_STATE

NP = BATCH * SEQ
NS = DEC_BATCH * DEC_SEQ
NTOK = NP + NS

LANES = 128
SUBLANES = 8
VMEM_LIMIT_BYTES = 56 * 1024 * 1024

FF_TILE = 512
FF_MAIN_TILES = FF_DIM // FF_TILE
FF_TAIL = FF_DIM - FF_MAIN_TILES * FF_TILE
FFN_TM = 512
TOK_TM = 512
OUT_ROW_SPLIT = 2
ATT_TQ = 1024
ATT_TK = 512
SSD_L = 128
QK_PAD = 256
V_AUG = V_DIM + 2 * SUBLANES
SSD_IN_TM = 1024
SSD_IN_TN = 1024
POOL_EXT = 16
ATT_SCALE = float((NOPE_DIM + ROPE_DIM) ** -0.5)
ATT_SCALE_LOG2E = ATT_SCALE * 1.4426950408889634


def _cparams(sem):
    return pltpu.CompilerParams(dimension_semantics=sem, vmem_limit_bytes=VMEM_LIMIT_BYTES)


def _layer_norm(y, g, b):
    mu = jnp.mean(y, axis=-1, keepdims=True)
    d = y - mu
    var = jnp.mean(d * d, axis=-1, keepdims=True)
    return d * lax.rsqrt(var + LN_EPS) * g + b


def _rms(h, g):
    return h * lax.rsqrt(jnp.mean(h * h, axis=-1, keepdims=True) + RMS_EPS) * g


def _silu(x):
    h = 0.5 * x
    return h + h * jnp.tanh(h)


def _dot(a, b):
    return jnp.dot(a, b, preferred_element_type=F32)


def _dot_nt(a, b):
    return lax.dot_general(a, b, (((1,), (1,)), ((), ())), preferred_element_type=F32)


def _dot_tn(a, b):
    return lax.dot_general(a, b, (((0,), (0,)), ((), ())), preferred_element_type=F32)


def _split2(v):
    hi = v.astype(BF16)
    lo = (v - hi.astype(F32)).astype(BF16)
    return hi, lo


def _split3(v):
    hi = v.astype(BF16)
    r = v - hi.astype(F32)
    mid = r.astype(BF16)
    lo = (r - mid.astype(F32)).astype(BF16)
    return hi, mid, lo


def _ffn_kernel(npt, n_in, n_out, *refs):
    x_refs = refs[:n_in]
    wg_ref, wu_ref, wd_ref, wgt_ref, wut_ref, wdt_ref, g_ref, b_ref = refs[n_in:n_in + 8]
    o_refs = refs[n_in + 8:n_in + 8 + n_out]
    xb_ref = refs[n_in + 8 + n_out]
    acc_ref = o_refs[0] if n_out == 1 else refs[n_in + 9 + n_out]
    i = pl.program_id(0)
    j = pl.program_id(1)

    def load_x():
        if n_in == 1:
            return x_refs[0][...]
        return jnp.where(i < npt, x_refs[0][...], x_refs[1][...])

    def partial_down(xb, wg, wu, wd):
        return _dot((_silu(_dot(xb, wg)) * _dot(xb, wu)).astype(BF16), wd)

    @pl.when(j == 0)
    def _():
        xb = load_x().astype(BF16)
        xb_ref[...] = xb
        acc_ref[...] = partial_down(xb, wgt_ref[0], wut_ref[0], wdt_ref[0])

    acc_ref[...] += partial_down(xb_ref[...], wg_ref[...], wu_ref[0], wd_ref[...])

    @pl.when(j == pl.num_programs(1) - 1)
    def _():
        y = _layer_norm(ALPHA * load_x() + 0.5 * acc_ref[...], g_ref[...], b_ref[...])
        if n_out == 1:
            o_refs[0][...] = y
        else:
            @pl.when(i < npt)
            def _():
                o_refs[0][...] = y

            @pl.when(i >= npt)
            def _():
                o_refs[1][...] = y


def _ffn(xs, w_up_b, w_down_b, layer, g, b, split_out=False):
    tm = FFN_TM
    npt = NP // tm
    nff = FF_MAIN_TILES
    tail0 = nff * FF_TILE
    el = pl.Element
    aligned = lambda v: pl.multiple_of(v, LANES)
    n_in = len(xs)
    n_out = 2 if split_out else 1
    prompt_tile = lambda i, j: (jnp.minimum(i, npt - 1), 0)
    sample_tile = lambda i, j: (jnp.maximum(i - npt, 0), 0)
    whole_tile = lambda i, j: (i, 0)
    x_specs = [pl.BlockSpec((tm, D_MODEL), m) for m in ((whole_tile,) if n_in == 1 else (prompt_tile, sample_tile))]
    if split_out:
        out_specs = [pl.BlockSpec((tm, D_MODEL), prompt_tile), pl.BlockSpec((tm, D_MODEL), sample_tile)]
        out_shape = [jax.ShapeDtypeStruct((NP, D_MODEL), F32), jax.ShapeDtypeStruct((NS, D_MODEL), F32)]
        scratch = [pltpu.VMEM((tm, D_MODEL), BF16), pltpu.VMEM((tm, D_MODEL), F32)]
    else:
        out_specs = pl.BlockSpec((tm, D_MODEL), whole_tile)
        out_shape = jax.ShapeDtypeStruct((NTOK, D_MODEL), F32)
        scratch = [pltpu.VMEM((tm, D_MODEL), BF16)]
    return pl.pallas_call(
        functools.partial(_ffn_kernel, npt, n_in, n_out),
        grid=(NTOK // tm, nff),
        in_specs=x_specs + [
            pl.BlockSpec((None, D_MODEL, FF_TILE), lambda i, j: (layer, 0, j)),
            pl.BlockSpec((el(1), el(D_MODEL), el(FF_TILE)), lambda i, j: (layer, 0, aligned(FF_DIM + j * FF_TILE))),
            pl.BlockSpec((None, FF_TILE, D_MODEL), lambda i, j: (layer, j, 0)),
            pl.BlockSpec((el(1), el(D_MODEL), el(FF_TAIL)), lambda i, j: (layer, 0, tail0), pipeline_mode=pl.Buffered(1)),
            pl.BlockSpec((el(1), el(D_MODEL), el(FF_TAIL)), lambda i, j: (layer, 0, FF_DIM + tail0), pipeline_mode=pl.Buffered(1)),
            pl.BlockSpec((el(1), el(FF_TAIL), el(D_MODEL)), lambda i, j: (layer, tail0, 0), pipeline_mode=pl.Buffered(1)),
            pl.BlockSpec((1, D_MODEL), lambda i, j: (0, 0)),
            pl.BlockSpec((1, D_MODEL), lambda i, j: (0, 0)),
        ],
        out_specs=out_specs,
        out_shape=out_shape,
        scratch_shapes=scratch,
        compiler_params=_cparams(("arbitrary", "arbitrary")),
        name="ffn_postnorm",
    )(*xs, w_up_b, w_up_b, w_down_b, w_up_b, w_up_b, w_down_b, g.reshape(1, -1), b.reshape(1, -1))


def _pool_window_mean_minus(ext_ref, base, rows, pos0):
    first = base + POOL_EXT
    t = lax.broadcasted_iota(jnp.int32, (rows, POOL_GROUP), 0)
    posp1 = (pos0 + t + 1).astype(F32)
    outs = []
    for g, w in enumerate(POOL_WINDOWS):
        cols = slice(g * POOL_GROUP, (g + 1) * POOL_GROUP)
        cur = ext_ref[pl.ds(first, rows), cols]
        tot = cur
        for s in range(1, w):
            tot = tot + ext_ref[pl.ds(first - s, rows), cols]
        outs.append(tot / jnp.minimum(posp1, float(w)) - cur)
    return outs


def _ab_in_kernel(n_prompt_tiles, x_ref, w_ref, past_ref, pw_ref, ps_ref, qg_ref, kg_ref, rope_ref,
                  ut_ref, yp_ref, qn_ref, ckvp_ref, ckvs_ref, kpep_ref, kpes_ref, ext_ref):
    i = pl.program_id(0)
    tm = x_ref.shape[0]
    h = _dot(x_ref[...].astype(BF16), w_ref[...])
    u = h[:, :POOL_DIM]
    for s in range(tm // DEC_SEQ):
        ut_ref[s] = u[(s + 1) * DEC_SEQ - POOL_EXT:(s + 1) * DEC_SEQ, :]
    qn_ref[...] = _rms(h[:, POOL_DIM:POOL_DIM + Q_LORA], qg_ref[...]).astype(BF16)
    ckv = _rms(h[:, POOL_DIM + Q_LORA:POOL_DIM + Q_LORA + KV_LORA], kg_ref[...])
    kv = h[:, POOL_DIM + Q_LORA + KV_LORA:] * rope_ref[...]
    kpe = (kv + pltpu.roll(kv, ROPE_DIM, axis=1))[:, :ROPE_DIM]

    @pl.when(i < n_prompt_tiles)
    def _():
        ckvp_ref[...] = ckv
        kpep_ref[...] = kpe

    @pl.when(i >= n_prompt_tiles)
    def _():
        ckvs_ref[...] = ckv
        kpes_ref[...] = kpe

    def finish(d_groups):
        for g in range(len(POOL_WINDOWS)):
            cols = slice(g * POOL_GROUP, (g + 1) * POOL_GROUP)
            y = _dot(d_groups[g].astype(BF16), pw_ref[g]) * ps_ref[:, cols]
            yp_ref[:, cols] = y.astype(BF16)

    @pl.when(i < n_prompt_tiles)
    def _():
        @pl.when(i == 0)
        def _():
            ext_ref[pl.ds(0, POOL_EXT), :] = jnp.zeros((POOL_EXT, POOL_DIM), F32)

        ext_ref[pl.ds(POOL_EXT, tm), :] = u
        finish(_pool_window_mean_minus(ext_ref, 0, tm, i * tm))
        ext_ref[pl.ds(0, POOL_EXT), :] = u[tm - POOL_EXT:, :]

    @pl.when(i >= n_prompt_tiles)
    def _():
        nseg = tm // DEC_SEQ
        stride = POOL_EXT + DEC_SEQ
        parts = [[] for _ in POOL_WINDOWS]
        for s in range(nseg):
            ext_ref[pl.ds(s * stride, POOL_EXT), :] = past_ref[s]
            ext_ref[pl.ds(s * stride + POOL_EXT, DEC_SEQ), :] = u[s * DEC_SEQ:(s + 1) * DEC_SEQ, :]
        for s in range(nseg):
            d = _pool_window_mean_minus(ext_ref, s * stride, DEC_SEQ, PAST_LEN)
            for g in range(len(POOL_WINDOWS)):
                parts[g].append(d[g])
        finish([jnp.concatenate(p, axis=0) for p in parts])


def _ab_in(x, w_in_x, pool_past, pool_w, pool_scale, q_norm_g, kv_norm_g, rope_k):
    n = x.shape[0]
    tm = TOK_TM
    nseg = tm // DEC_SEQ
    npt = NP // tm
    wcols = w_in_x.shape[1]
    ext_rows = max(tm + POOL_EXT, nseg * (POOL_EXT + DEC_SEQ))
    const = lambda i: (0, 0)
    prompt_tile = lambda i: (jnp.minimum(i, npt - 1), 0)
    sample_tile = lambda i: (jnp.maximum(i - npt, 0), 0)
    return pl.pallas_call(
        functools.partial(_ab_in_kernel, npt),
        grid=(n // tm,),
        in_specs=[
            pl.BlockSpec((tm, D_MODEL), lambda i: (i, 0)),
            pl.BlockSpec((D_MODEL, wcols), const),
            pl.BlockSpec((nseg, POOL_EXT, POOL_DIM), lambda i: (jnp.maximum(i - npt, 0), 0, 0)),
            pl.BlockSpec((len(POOL_WINDOWS), POOL_GROUP, POOL_GROUP), lambda i: (0, 0, 0)),
            pl.BlockSpec((1, POOL_DIM), const),
            pl.BlockSpec((1, Q_LORA), const),
            pl.BlockSpec((1, KV_LORA), const),
            pl.BlockSpec((tm, LANES), lambda i: (i, 0)),
        ],
        out_specs=[
            pl.BlockSpec((nseg, POOL_EXT, POOL_DIM), lambda i: (i, 0, 0)),
            pl.BlockSpec((tm, POOL_DIM), lambda i: (i, 0)),
            pl.BlockSpec((tm, Q_LORA), lambda i: (i, 0)),
            pl.BlockSpec((tm, KV_LORA), prompt_tile),
            pl.BlockSpec((tm, KV_LORA), sample_tile),
            pl.BlockSpec((tm, ROPE_DIM), prompt_tile),
            pl.BlockSpec((tm, ROPE_DIM), sample_tile),
        ],
        out_shape=[
            jax.ShapeDtypeStruct((n // DEC_SEQ, POOL_EXT, POOL_DIM), F32),
            jax.ShapeDtypeStruct((n, POOL_DIM), BF16),
            jax.ShapeDtypeStruct((n, Q_LORA), BF16),
            jax.ShapeDtypeStruct((NP, KV_LORA), F32),
            jax.ShapeDtypeStruct((NS, KV_LORA), F32),
            jax.ShapeDtypeStruct((NP, ROPE_DIM), F32),
            jax.ShapeDtypeStruct((NS, ROPE_DIM), F32),
        ],
        scratch_shapes=[pltpu.VMEM((ext_rows, POOL_DIM), F32)],
        compiler_params=_cparams(("arbitrary",)),
        name="ab_in",
    )(x, w_in_x, pool_past, pool_w.astype(BF16), pool_scale.reshape(1, -1),
      q_norm_g.reshape(1, -1), kv_norm_g.reshape(1, -1), rope_k)


def _q_up_kernel(qn_ref, w_ref, rope_ref, q_ref):
    qn = qn_ref[...]
    rope = rope_ref[...]
    lane = lax.broadcasted_iota(jnp.int32, rope.shape, 1)
    hpc = 4
    ncol = hpc * LANES
    for c in range(MLA_HEADS // hpc):
        nope = _dot(qn, w_ref[:, c * ncol:(c + 1) * ncol])
        rp = _dot(qn, w_ref[:, MLA_HEADS * LANES + c * ncol:MLA_HEADS * LANES + (c + 1) * ncol])
        for hh in range(hpc):
            hd = c * hpc + hh
            v = rp[:, hh * LANES:(hh + 1) * LANES] * rope
            rot = jnp.where(lane < ROPE_DIM, v + pltpu.roll(v, ROPE_DIM, axis=1), 0.0)
            q_ref[hd, :, 0:LANES] = (nope[:, hh * LANES:(hh + 1) * LANES] * ATT_SCALE).astype(BF16)
            q_ref[hd, :, LANES:QK_PAD] = (rot * ATT_SCALE).astype(BF16)


def _q_up(qn, wq_x, rope_q, row0, nrows):
    tm = TOK_TM
    t0 = row0 // tm
    return pl.pallas_call(
        _q_up_kernel,
        grid=(nrows // tm,),
        in_specs=[
            pl.BlockSpec((tm, Q_LORA), lambda i: (t0 + i, 0)),
            pl.BlockSpec(wq_x.shape, lambda i: (0, 0)),
            pl.BlockSpec((tm, LANES), lambda i: (t0 + i, 0)),
        ],
        out_specs=pl.BlockSpec((MLA_HEADS, tm, QK_PAD), lambda i: (0, i, 0)),
        out_shape=jax.ShapeDtypeStruct((MLA_HEADS, nrows, QK_PAD), BF16),
        compiler_params=_cparams(("parallel",)),
        name="q_up",
    )(qn, wq_x, rope_q)


def _q_up_t_kernel(qn_ref, wt_ref, ropet_ref, qt_ref):
    qt_all = _dot_nt(wt_ref[...], qn_ref[...])
    ropet = ropet_ref[...]
    nrope = MLA_HEADS * LANES
    zeros = jnp.zeros((QK_PAD - LANES - ROPE_DIM, qt_all.shape[1]), BF16)
    for hd in range(MLA_HEADS):
        v = qt_all[nrope + hd * LANES:nrope + (hd + 1) * LANES, :] * ropet
        rot = v[:ROPE_DIM] + v[ROPE_DIM:]
        qt_ref[hd, 0, 0:LANES, :] = (qt_all[hd * LANES:(hd + 1) * LANES, :] * ATT_SCALE_LOG2E).astype(BF16)
        qt_ref[hd, 0, LANES:LANES + ROPE_DIM, :] = (rot * ATT_SCALE_LOG2E).astype(BF16)
        qt_ref[hd, 0, LANES + ROPE_DIM:QK_PAD, :] = zeros


def _q_up_t(qn, wq_xt, rope_t, nrows):
    tm = ATT_TQ
    return pl.pallas_call(
        _q_up_t_kernel,
        grid=(nrows // tm,),
        in_specs=[
            pl.BlockSpec((tm, Q_LORA), lambda i: (i, 0)),
            pl.BlockSpec(wq_xt.shape, lambda i: (0, 0)),
            pl.BlockSpec((LANES, tm), lambda i: (0, i)),
        ],
        out_specs=pl.BlockSpec((MLA_HEADS, 1, QK_PAD, tm), lambda i: (0, i, 0, 0)),
        out_shape=jax.ShapeDtypeStruct((MLA_HEADS, nrows // tm, QK_PAD, tm), BF16),
        compiler_params=_cparams(("parallel",)),
        name="q_up_t",
    )(qn, wq_xt, rope_t)


def _kv_up_kernel(ckv_ref, kpe_ref, w_ref, k_ref, v_ref):
    c = ckv_ref[...].astype(BF16)
    kpe = kpe_ref[...].astype(BF16)
    hpc = 4
    ncol = hpc * LANES
    for j in range(MLA_HEADS // hpc):
        kn = _dot(c, w_ref[:, j * ncol:(j + 1) * ncol])
        vv = _dot(c, w_ref[:, MLA_HEADS * LANES + j * ncol:MLA_HEADS * LANES + (j + 1) * ncol])
        for hh in range(hpc):
            hd = j * hpc + hh
            k_ref[hd, :, 0:LANES] = kn[:, hh * LANES:(hh + 1) * LANES].astype(BF16)
            k_ref[hd, :, LANES:LANES + ROPE_DIM] = kpe
            k_ref[hd, :, LANES + ROPE_DIM:QK_PAD] = jnp.zeros_like(kpe)
            v_ref[hd] = vv[:, hh * LANES:(hh + 1) * LANES].astype(BF16)


def _kv_up(ckv, kpe, wkv_x):
    tm = TOK_TM
    nrows = ckv.shape[0]
    return pl.pallas_call(
        _kv_up_kernel,
        grid=(nrows // tm,),
        in_specs=[
            pl.BlockSpec((tm, KV_LORA), lambda i: (i, 0)),
            pl.BlockSpec((tm, ROPE_DIM), lambda i: (i, 0)),
            pl.BlockSpec(wkv_x.shape, lambda i: (0, 0)),
        ],
        out_specs=[
            pl.BlockSpec((MLA_HEADS, tm, QK_PAD), lambda i: (0, i, 0)),
            pl.BlockSpec((MLA_HEADS, tm, V_DIM), lambda i: (0, i, 0)),
        ],
        out_shape=[
            jax.ShapeDtypeStruct((MLA_HEADS, nrows, QK_PAD), BF16),
            jax.ShapeDtypeStruct((MLA_HEADS, nrows, V_DIM), BF16),
        ],
        compiler_params=_cparams(("parallel",)),
        name="kv_up",
    )(ckv, kpe, wkv_x)


def _kv_up_t_kernel(ckv_ref, kpe_ref, wk_ref, wvt_ref, k_ref, vt_ref):
    c = ckv_ref[...].astype(BF16)
    kpe = kpe_ref[...].astype(BF16)
    hpc = 4
    ncol = hpc * LANES
    for j in range(MLA_HEADS // hpc):
        kn = _dot(c, wk_ref[:, j * ncol:(j + 1) * ncol])
        for hh in range(hpc):
            hd = j * hpc + hh
            k_ref[hd, :, 0:LANES] = kn[:, hh * LANES:(hh + 1) * LANES].astype(BF16)
            k_ref[hd, :, LANES:LANES + ROPE_DIM] = kpe
            k_ref[hd, :, LANES + ROPE_DIM:QK_PAD] = jnp.zeros_like(kpe)
    vt_all = _dot_nt(wvt_ref[...], c)
    ones = jnp.ones((V_AUG - V_DIM, vt_all.shape[1]), BF16)
    for hd in range(MLA_HEADS):
        vt_ref[hd, 0, 0:V_DIM, :] = vt_all[hd * V_DIM:(hd + 1) * V_DIM, :].astype(BF16)
        vt_ref[hd, 0, V_DIM:V_AUG, :] = ones


def _kv_up_t(ckv, kpe, wk, wvt):
    tm = ATT_TK
    nrows = ckv.shape[0]
    return pl.pallas_call(
        _kv_up_t_kernel,
        grid=(nrows // tm,),
        in_specs=[
            pl.BlockSpec((tm, KV_LORA), lambda i: (i, 0)),
            pl.BlockSpec((tm, ROPE_DIM), lambda i: (i, 0)),
            pl.BlockSpec(wk.shape, lambda i: (0, 0)),
            pl.BlockSpec(wvt.shape, lambda i: (0, 0)),
        ],
        out_specs=[
            pl.BlockSpec((MLA_HEADS, tm, QK_PAD), lambda i: (0, i, 0)),
            pl.BlockSpec((MLA_HEADS, 1, V_AUG, tm), lambda i: (0, i, 0, 0)),
        ],
        out_shape=[
            jax.ShapeDtypeStruct((MLA_HEADS, nrows, QK_PAD), BF16),
            jax.ShapeDtypeStruct((MLA_HEADS, nrows // tm, V_AUG, tm), BF16),
        ],
        compiler_params=_cparams(("parallel",)),
        name="kv_up_t",
    )(ckv, kpe, wk, wvt)


def _attn_prompt_kernel(qt_ref, k_ref, vt_ref, o_ref, s0_ref, s1_ref, m_ref, acc_ref):
    i = pl.program_id(1)
    tk = s0_ref.shape[0]
    tq = qt_ref.shape[3]

    def scores(c, s_ref):
        start = pl.multiple_of(c * tk, tk)
        s_ref[...] = _dot(k_ref[0, pl.ds(start, tk), :], qt_ref[0, 0])

    def softmax_pv(c, s_ref, mask):
        st = s_ref[...]
        if mask is not None:
            st = jnp.where(mask, st, -jnp.inf)
        m = m_ref[...]
        m_new = jnp.maximum(m, jnp.max(st, axis=0, keepdims=True))
        alpha = jnp.exp2(m - m_new)
        p = jnp.exp2(st - m_new)
        m_ref[...] = m_new
        acc_ref[...] = alpha * acc_ref[...] + _dot(vt_ref[0, c], p.astype(BF16))

    m_ref[...] = jnp.full(m_ref.shape, -jnp.inf, F32)
    acc_ref[...] = jnp.zeros(acc_ref.shape, F32)
    key_chunk = lax.broadcasted_iota(jnp.int32, (tk, tq), 0) // CHUNK
    query_chunk = lax.broadcasted_iota(jnp.int32, (tk, tq), 1) // CHUNK

    scores(0, s0_ref)

    def pair(j, carry):
        scores(2 * j + 1, s1_ref)
        softmax_pv(2 * j, s0_ref, None)
        scores(2 * j + 2, s0_ref)
        softmax_pv(2 * j + 1, s1_ref, None)
        return carry

    lax.fori_loop(0, i, pair, 0)
    scores(2 * i + 1, s1_ref)
    softmax_pv(2 * i, s0_ref, query_chunk >= key_chunk)
    softmax_pv(2 * i + 1, s1_ref, query_chunk >= key_chunk + tk // CHUNK)

    o_ref[...] = (acc_ref[0:V_DIM, :] / acc_ref[V_DIM:V_DIM + 1, :]).T.astype(BF16)


def _attn_prompt(qt, k, vt):
    tq, tk = ATT_TQ, ATT_TK
    assert tq == 2 * tk and tk % CHUNK == 0
    score = pltpu.VMEM((tk, tq), F32)
    return pl.pallas_call(
        _attn_prompt_kernel,
        grid=(MLA_HEADS, NP // tq),
        in_specs=[
            pl.BlockSpec((1, 1, QK_PAD, tq), lambda h, i: (h, i, 0, 0)),
            pl.BlockSpec((1, NP, QK_PAD), lambda h, i: (h, 0, 0)),
            pl.BlockSpec((1, NP // tk, V_AUG, tk), lambda h, i: (h, 0, 0, 0)),
        ],
        out_specs=pl.BlockSpec((tq, V_DIM), lambda h, i: (i, h)),
        out_shape=jax.ShapeDtypeStruct((NP, MLA_HEADS * V_DIM), BF16),
        scratch_shapes=[score, score, pltpu.VMEM((1, tq), F32), pltpu.VMEM((V_AUG, tq), F32)],
        compiler_params=_cparams(("parallel", "arbitrary")),
        name="attn_prompt",
    )(qt, k, vt)


def _attn_sample_kernel(q_ref, kp_ref, vp_ref, kn_ref, vn_ref, o_ref):
    for hd in range(MLA_HEADS):
        q = q_ref[hd]
        sp = _dot_nt(q, kp_ref[hd])
        sn = _dot_nt(q, kn_ref[hd])
        m = jnp.maximum(jnp.max(sp, axis=1, keepdims=True), jnp.max(sn, axis=1, keepdims=True))
        pp = jnp.exp(sp - m)
        pn = jnp.exp(sn - m)
        l = jnp.sum(pp, axis=1, keepdims=True) + jnp.sum(pn, axis=1, keepdims=True)
        o = _dot(pp.astype(BF16), vp_ref[hd]) + _dot(pn.astype(BF16), vn_ref[hd])
        o_ref[:, hd * V_DIM:(hd + 1) * V_DIM] = (o / l).astype(BF16)


def _attn_sample(q, k_past, v_past, k_new, v_new):
    assert PAST_LEN % CHUNK == 0 and DEC_SEQ <= CHUNK
    return pl.pallas_call(
        _attn_sample_kernel,
        grid=(DEC_BATCH,),
        in_specs=[
            pl.BlockSpec((MLA_HEADS, DEC_SEQ, QK_PAD), lambda b: (0, b, 0)),
            pl.BlockSpec((MLA_HEADS, PAST_LEN, QK_PAD), lambda b: (0, b, 0)),
            pl.BlockSpec((MLA_HEADS, PAST_LEN, V_DIM), lambda b: (0, b, 0)),
            pl.BlockSpec((MLA_HEADS, DEC_SEQ, QK_PAD), lambda b: (0, b, 0)),
            pl.BlockSpec((MLA_HEADS, DEC_SEQ, V_DIM), lambda b: (0, b, 0)),
        ],
        out_specs=pl.BlockSpec((DEC_SEQ, MLA_HEADS * V_DIM), lambda b: (b, 0)),
        out_shape=jax.ShapeDtypeStruct((NS, MLA_HEADS * V_DIM), BF16),
        compiler_params=_cparams(("parallel",)),
        name="attn_sample",
    )(q, k_past, v_past, k_new, v_new)


def _ab_out_kernel(npt, x_ref, yp_ref, ap_ref, as_ref, wp_ref, wa_ref, g_ref, b_ref, o_ref):
    i = pl.program_id(0)
    half = x_ref.shape[0] // OUT_ROW_SPLIT
    for r in range(OUT_ROW_SPLIT):
        rows = pl.ds(r * half, half)
        att = jnp.where(i < npt, ap_ref[rows, :], as_ref[rows, :])
        mix = _dot(yp_ref[rows, :], wp_ref[...]) + _dot(att, wa_ref[...])
        o_ref[rows, :] = _layer_norm(ALPHA * x_ref[rows, :] + mix, g_ref[...], b_ref[...])


def _ab_out(x, ypool, att_p, att_s, w_out, g, b):
    n = x.shape[0]
    tm = TOK_TM
    npt = NP // tm
    w = w_out.astype(BF16)
    adim = MLA_HEADS * V_DIM
    const = lambda i: (0, 0)
    return pl.pallas_call(
        functools.partial(_ab_out_kernel, npt),
        grid=(n // tm,),
        in_specs=[
            pl.BlockSpec((tm, D_MODEL), lambda i: (i, 0)),
            pl.BlockSpec((tm, POOL_DIM), lambda i: (i, 0)),
            pl.BlockSpec((tm, adim), lambda i: (jnp.minimum(i, npt - 1), 0)),
            pl.BlockSpec((tm, adim), lambda i: (jnp.maximum(i - npt, 0), 0)),
            pl.BlockSpec((POOL_DIM, D_MODEL), const),
            pl.BlockSpec((adim, D_MODEL), const),
            pl.BlockSpec((1, D_MODEL), const),
            pl.BlockSpec((1, D_MODEL), const),
        ],
        out_specs=pl.BlockSpec((tm, D_MODEL), lambda i: (i, 0)),
        out_shape=jax.ShapeDtypeStruct((n, D_MODEL), F32),
        compiler_params=_cparams(("parallel",)),
        name="ab_out",
    )(x, ypool, att_p, att_s, w[:POOL_DIM], w[POOL_DIM:], g.reshape(1, -1), b.reshape(1, -1))


def _ssd_out_kernel(npt, x_ref, ap_ref, as_ref, w_ref, g_ref, b_ref, o_ref):
    i = pl.program_id(0)
    half = x_ref.shape[0] // OUT_ROW_SPLIT
    for r in range(OUT_ROW_SPLIT):
        rows = pl.ds(r * half, half)
        act = jnp.where(i < npt, ap_ref[rows, :], as_ref[rows, :])
        o_ref[rows, :] = _layer_norm(ALPHA * x_ref[rows, :] + _dot(act, w_ref[...]), g_ref[...], b_ref[...])


def _ssd_out(x, act_p, act_s, w_out, g, b):
    n = x.shape[0]
    tm = TOK_TM
    npt = NP // tm
    const = lambda i: (0, 0)
    return pl.pallas_call(
        functools.partial(_ssd_out_kernel, npt),
        grid=(n // tm,),
        in_specs=[
            pl.BlockSpec((tm, D_MODEL), lambda i: (i, 0)),
            pl.BlockSpec((tm, SSM_INNER), lambda i: (jnp.minimum(i, npt - 1), 0)),
            pl.BlockSpec((tm, SSM_INNER), lambda i: (jnp.maximum(i - npt, 0), 0)),
            pl.BlockSpec((SSM_INNER, D_MODEL), const),
            pl.BlockSpec((1, D_MODEL), const),
            pl.BlockSpec((1, D_MODEL), const),
        ],
        out_specs=pl.BlockSpec((tm, D_MODEL), lambda i: (i, 0)),
        out_shape=jax.ShapeDtypeStruct((n, D_MODEL), F32),
        compiler_params=_cparams(("parallel",)),
        name="ssd_out",
    )(x, act_p, act_s, w_out.astype(BF16), g.reshape(1, -1), b.reshape(1, -1))


def _ssd_in_kernel(x_ref, w_ref, wdt_ref, o_ref, dt_ref, xb_ref):
    @pl.when(pl.program_id(1) == 0)
    def _():
        xb_ref[...] = x_ref[...].astype(BF16)
        dt_ref[...] = _dot(xb_ref[...], wdt_ref[...])

    o_ref[...] = _dot(xb_ref[...], w_ref[...])


def _ssd_in(x, w_all, w_dt):
    n = x.shape[0]
    tm = SSD_IN_TM
    ncol = SSM_INNER + CONV_DIM
    w_zx = w_all
    return pl.pallas_call(
        _ssd_in_kernel,
        grid=(n // tm, ncol // SSD_IN_TN),
        in_specs=[
            pl.BlockSpec((tm, D_MODEL), lambda i, j: (i, 0)),
            pl.BlockSpec((D_MODEL, SSD_IN_TN), lambda i, j: (0, j)),
            pl.BlockSpec((D_MODEL, LANES), lambda i, j: (0, 0)),
        ],
        out_specs=[
            pl.BlockSpec((tm, SSD_IN_TN), lambda i, j: (i, j)),
            pl.BlockSpec((tm, LANES), lambda i, j: (i, 0)),
        ],
        out_shape=[
            jax.ShapeDtypeStruct((n, ncol), F32),
            jax.ShapeDtypeStruct((n, LANES), F32),
        ],
        scratch_shapes=[pltpu.VMEM((tm, D_MODEL), BF16)],
        compiler_params=_cparams(("parallel", "arbitrary")),
        name="ssd_in",
    )(x, w_zx, w_dt)


def _ssd_kernel(rows, fresh_each_step, z_ref, xs_ref, bc_ref, dt_ref, cpast_ref, h0_ref, cw_ref,
                cb_ref, dtb_ref, a_ref, dsk_ref, ng_ref, e_ref, y_ref, h_ref, ext_ref):
    c = pl.program_id(0)
    L = SSD_L
    G, R, P, S = SSM_GROUPS, SSM_HPG, SSM_HEAD_DIM, SSM_STATE
    halo = SUBLANES

    def start_sequence():
        ext_ref[pl.ds(0, halo), :] = cpast_ref[0]
        h_ref[0] = h0_ref[0]

    if fresh_each_step:
        start_sequence()
    else:
        pl.when(c == 0)(start_sequence)

    ext_ref[pl.ds(halo, rows), 0:SSM_INNER] = xs_ref[...]
    ext_ref[pl.ds(halo, rows), SSM_INNER:CONV_DIM] = bc_ref[...]
    if rows < L:
        ext_ref[pl.ds(halo + rows, L - rows), :] = jnp.zeros((L - rows, CONV_DIM), F32)

    assert CONV_WIDTH == 4
    win = ext_ref[pl.ds(0, halo + L), :]
    prev = pltpu.roll(win, 1, axis=0)
    near = cw_ref[3:4, :] * win[halo:] + cw_ref[2:3, :] * prev[halo:]
    far = cw_ref[1:2, :] * win + cw_ref[0:1, :] * prev
    conv = cb_ref[...] + near + pltpu.roll(far, 2, axis=0)[halo:]
    ext_ref[pl.ds(0, halo), :] = ext_ref[pl.ds(rows, halo), :]
    act = _silu(conv)
    xs = act[:, :SSM_INNER]
    xsb = xs.astype(BF16)
    bmat = act[:, SSM_INNER:SSM_INNER + G * S].astype(BF16)
    cmat = act[:, SSM_INNER + G * S:].astype(BF16)

    dtr = dt_ref[...] + dtb_ref[...]
    dt = jnp.maximum(dtr, 0.0) + jnp.log1p(jnp.exp(-jnp.abs(dtr)))
    if rows < L:
        dt = jnp.concatenate([dt, jnp.zeros((L - rows, LANES), F32)], axis=0)
    da = dt * a_ref[...]
    ti = lax.broadcasted_iota(jnp.int32, (L, L), 0)
    si = lax.broadcasted_iota(jnp.int32, (L, L), 1)
    causal = ti >= si
    tri = jnp.where(causal, 1.0, 0.0).astype(BF16)
    acs = sum(_dot(tri, part) for part in _split3(da))
    acs_t = acs.T
    dt_t = dt.T
    last = acs[L - 1:L, :]
    to_end = jnp.exp(last - acs) * dt
    eacs = jnp.exp(acs)
    onehot = e_ref[...]
    expand = lambda v: sum(_dot(part, onehot) for part in _split2(v))
    x_end = (xs * expand(to_end)).astype(BF16)
    eacs_x = expand(eacs)
    dec_x = sum(_dot(part, onehot) for part in _split3(jnp.broadcast_to(jnp.exp(last), (SUBLANES, LANES))))[0:1, :]

    lane4 = lax.broadcasted_iota(jnp.int32, (L, 4 * P), 1) // P
    for g in range(G):
        bg = bmat[:, g * S:(g + 1) * S]
        cg = cmat[:, g * S:(g + 1) * S]
        cbm = _dot_nt(cg, bg)
        cols = slice(g * R * P, (g + 1) * R * P)
        ht = h_ref[0, :, cols]
        y_state = _dot(cg, ht.astype(BF16)) * eacs_x[:, cols]
        y_parts = []
        for half in range(R // 4):
            x4 = xsb[:, g * R * P + half * 4 * P:g * R * P + (half + 1) * 4 * P]
            x_bd = jnp.concatenate([jnp.where(lane4 == r, x4, jnp.zeros_like(x4)) for r in range(4)], axis=0)
            wts = []
            for r in range(4):
                hd = g * R + half * 4 + r
                seg = acs[:, hd:hd + 1] - acs_t[hd:hd + 1, :]
                decay = jnp.exp(jnp.where(causal, seg, -jnp.inf))
                wts.append((cbm * decay * dt_t[hd:hd + 1, :]).astype(BF16))
            y_parts.append(_dot(jnp.concatenate(wts, axis=1), x_bd))
        y = jnp.concatenate(y_parts, axis=1) + y_state + dsk_ref[:, cols] * xs[:, cols]
        gz = y[:rows] * _silu(z_ref[:, cols])
        gz = gz * lax.rsqrt(jnp.mean(gz * gz, axis=-1, keepdims=True) + RMS_EPS)
        y_ref[:, cols] = (gz * ng_ref[:, cols]).astype(BF16)
        h_ref[0, :, cols] = ht * dec_x[:, cols] + _dot_tn(bg, x_end[:, cols])


def _ssd_core(zx, dtx, row0, nblk, rows, fresh_each_step, conv_past8, h0_t, conv_w8, conv_b, dt_bias, a_neg,
              d_skip_x, norm_g, onehot):
    L = SSD_L
    r0 = row0 // rows
    nseq = nblk if fresh_each_step else 1
    seq = (lambda c: c) if fresh_each_step else (lambda c: 0)
    const = lambda c: (0, 0)
    return pl.pallas_call(
        functools.partial(_ssd_kernel, rows, fresh_each_step),
        grid=(nblk,),
        in_specs=[
            pl.BlockSpec((rows, SSM_INNER), lambda c: (r0 + c, 0)),
            pl.BlockSpec((rows, SSM_INNER), lambda c: (r0 + c, 1)),
            pl.BlockSpec((rows, 2 * SSM_GROUPS * SSM_STATE), lambda c: (r0 + c, 4)),
            pl.BlockSpec((rows, LANES), lambda c: (r0 + c, 0)),
            pl.BlockSpec((1, SUBLANES, CONV_DIM), lambda c: (seq(c), 0, 0)),
            pl.BlockSpec((1, SSM_STATE, SSM_INNER), lambda c: (seq(c), 0, 0)),
            pl.BlockSpec((SUBLANES, CONV_DIM), const),
            pl.BlockSpec((1, CONV_DIM), const),
            pl.BlockSpec((1, LANES), const),
            pl.BlockSpec((1, LANES), const),
            pl.BlockSpec((1, SSM_INNER), const),
            pl.BlockSpec((1, SSM_INNER), const),
            pl.BlockSpec((LANES, SSM_INNER), const),
        ],
        out_specs=[
            pl.BlockSpec((rows, SSM_INNER), lambda c: (c, 0)),
            pl.BlockSpec((1, SSM_STATE, SSM_INNER), lambda c: (seq(c), 0, 0)),
        ],
        out_shape=[
            jax.ShapeDtypeStruct((nblk * rows, SSM_INNER), BF16),
            jax.ShapeDtypeStruct((nseq, SSM_STATE, SSM_INNER), F32),
        ],
        scratch_shapes=[pltpu.VMEM((L + 2 * SUBLANES, CONV_DIM), F32)],
        compiler_params=_cparams(("arbitrary",)),
        name="ssd_core_seq" if not fresh_each_step else "ssd_core_blocks",
    )(zx, zx, zx, dtx, conv_past8, h0_t, conv_w8, conv_b, dt_bias, a_neg, d_skip_x, norm_g, onehot)


def _rope_table():
    half = ROPE_DIM // 2
    inv = (1.0 / (np.float32(ROPE_THETA) ** (np.arange(half, dtype=np.float32) * np.float32(2.0) / np.float32(ROPE_DIM)))).astype(np.float32)
    pos = np.concatenate([
        np.tile(np.arange(SEQ, dtype=np.float32), BATCH),
        np.tile(np.arange(DEC_SEQ, dtype=np.float32) + np.float32(PAST_LEN), DEC_BATCH)])
    ang = (pos[:, None] * inv[None, :]).astype(np.float32).astype(np.float64)
    cos, sin = np.cos(ang), np.sin(ang)
    return np.concatenate([cos, cos, -sin, sin], axis=1).astype(np.float32)


def _swap_halves(w):
    half = w.shape[-1] // 2
    return jnp.concatenate([w[..., half:], w[..., :half]], axis=-1)


def _prep_ab_weights(w_in_ab, w_q_up, w_kv_up):
    kpe_w = w_in_ab[:, POOL_DIM + Q_LORA + KV_LORA:]
    w_in_x = jnp.concatenate([w_in_ab, _swap_halves(kpe_w)], axis=1).astype(BF16)
    wq = w_q_up.reshape(Q_LORA, MLA_HEADS, NOPE_DIM + ROPE_DIM)
    wq_rope = wq[..., NOPE_DIM:]
    wq_x = jnp.concatenate([
        wq[..., :NOPE_DIM].reshape(Q_LORA, -1),
        jnp.concatenate([wq_rope, _swap_halves(wq_rope)], axis=-1).reshape(Q_LORA, -1)], axis=1).astype(BF16)
    wkv = w_kv_up.reshape(KV_LORA, MLA_HEADS, NOPE_DIM + V_DIM)
    wk = wkv[..., :NOPE_DIM].reshape(KV_LORA, -1).astype(BF16)
    wv = wkv[..., NOPE_DIM:].reshape(KV_LORA, -1).astype(BF16)
    return w_in_x, wq_x, wk, wv


def _state_to_t(h):
    b = h.shape[0]
    return jnp.transpose(h, (0, 3, 1, 2)).reshape(b, SSM_STATE, SSM_INNER)


def _state_from_t(ht):
    b = ht.shape[0]
    return jnp.transpose(ht.reshape(b, SSM_STATE, SSM_HEADS, SSM_HEAD_DIM), (0, 2, 3, 1))


def kernel(x_prompt, x_sample, state_pool, cache_ckv, cache_kpe, state_conv, state_ssm, ffn_pre_up, ffn_pre_down, ffn_post_up, ffn_post_down, ln_g, ln_b, w_in_ab, pool_w, pool_scale, q_norm_g, w_q_up, kv_norm_g, w_kv_up, w_out_ab, w_in_ssd, conv_w, conv_b, dt_bias, a_log, d_skip, ssm_norm_g, w_out_ssd):
    pre_up, pre_down = ffn_pre_up.astype(BF16), ffn_pre_down.astype(BF16)
    post_up, post_down = ffn_post_up.astype(BF16), ffn_post_down.astype(BF16)

    x = _ffn((x_prompt.reshape(NP, D_MODEL), x_sample.reshape(NS, D_MODEL)), pre_up, pre_down, 0,
             ln_g[0, 0], ln_b[0, 0])
    rope = _rope_table()
    w_in_x, wq_x, wk, wv = _prep_ab_weights(w_in_ab, w_q_up, w_kv_up)
    wkv_x = jnp.concatenate([wk, wv], axis=1)
    pool_past = jnp.pad(state_pool, ((0, 0), (POOL_EXT - POOL_STATE, 0), (0, 0)))
    utail, ypool, qn, ckv_p, ckv_s, kpe_p, kpe_s = _ab_in(
        x, w_in_x, pool_past, pool_w, pool_scale, q_norm_g, kv_norm_g, rope)
    qt_p = _q_up_t(qn, wq_x.T, np.ascontiguousarray(rope[:NP].T), NP)
    k_p, vt_p = _kv_up_t(ckv_p, kpe_p, wk, wv.T)
    q_s = _q_up(qn, wq_x, rope, NP, NS)
    k_s, v_s = _kv_up(ckv_s, kpe_s, wkv_x)
    k_past, v_past = _kv_up(cache_ckv.reshape(DEC_BATCH * PAST_LEN, KV_LORA),
                            cache_kpe.reshape(DEC_BATCH * PAST_LEN, ROPE_DIM), wkv_x)
    att_p = _attn_prompt(qt_p, k_p, vt_p)
    att_s = _attn_sample(q_s, k_past, v_past, k_s, v_s)
    x = _ab_out(x, ypool, att_p, att_s, w_out_ab, ln_g[0, 1], ln_b[0, 1])
    x = _ffn((x,), post_up, post_down, 0, ln_g[0, 2], ln_b[0, 2])

    skip = POOL_EXT - POOL_STATE
    pool_p = utail[:NP // DEC_SEQ].reshape(BATCH, SEQ // DEC_SEQ, POOL_EXT, POOL_DIM)[:, -1, skip:]
    pool_s = utail[NP // DEC_SEQ:, skip:]
    ckv_p = ckv_p.reshape(BATCH, SEQ, KV_LORA)
    ckv_s = ckv_s.reshape(DEC_BATCH, DEC_SEQ, KV_LORA)
    kpe_p = kpe_p.reshape(BATCH, SEQ, ROPE_DIM)
    kpe_s = kpe_s.reshape(DEC_BATCH, DEC_SEQ, ROPE_DIM)

    x = _ffn((x,), pre_up, pre_down, 1, ln_g[1, 0], ln_b[1, 0])
    nzx = SSM_INNER + CONV_DIM
    w_ssd = w_in_ssd.astype(BF16)
    zx, dtx = _ssd_in(x, w_ssd, jnp.pad(w_ssd[:, nzx:], ((0, 0), (0, LANES - SSM_HEADS))))
    conv_w8 = jnp.pad(conv_w, ((0, SUBLANES - CONV_WIDTH), (0, 0)))
    pad_lanes = lambda v, fill: jnp.pad(v.astype(F32), (0, LANES - v.shape[0]), constant_values=fill).reshape(1, LANES)
    a_neg = -jnp.exp(pad_lanes(a_log, 0.0))
    d_skip_x = jnp.repeat(d_skip.astype(F32), SSM_HEAD_DIM).reshape(1, SSM_INNER)
    onehot = (jnp.arange(LANES)[:, None] == (jnp.arange(SSM_INNER) // SSM_HEAD_DIM)[None, :]).astype(BF16)
    common = (conv_w8, conv_b.reshape(1, -1), pad_lanes(dt_bias, 0.0), a_neg, d_skip_x,
              ssm_norm_g.reshape(1, -1), onehot)
    hist = SUBLANES - (CONV_WIDTH - 1)
    y_p, h_p = _ssd_core(zx, dtx, 0, NP // SSD_L, SSD_L, False,
                         jnp.zeros((BATCH, SUBLANES, CONV_DIM), F32),
                         jnp.zeros((BATCH, SSM_STATE, SSM_INNER), F32), *common)
    y_s, h_s = _ssd_core(zx, dtx, NP, DEC_BATCH, DEC_SEQ, True,
                         jnp.pad(state_conv, ((0, 0), (hist, 0), (0, 0))),
                         _state_to_t(state_ssm), *common)
    x = _ssd_out(x, y_p, y_s, w_out_ssd, ln_g[1, 1], ln_b[1, 1])
    y_prompt, y_sample = _ffn((x,), post_up, post_down, 1, ln_g[1, 2], ln_b[1, 2], split_out=True)

    tail = CONV_WIDTH - 1
    conv_p = jnp.stack([zx[(b + 1) * SEQ - tail:(b + 1) * SEQ, SSM_INNER:] for b in range(BATCH)])
    conv_s = jnp.stack([zx[NP + (b + 1) * DEC_SEQ - tail:NP + (b + 1) * DEC_SEQ, SSM_INNER:]
                        for b in range(DEC_BATCH)])
    ssm_p = _state_from_t(h_p)
    ssm_s = _state_from_t(h_s)

    return (y_prompt.reshape(BATCH, SEQ, D_MODEL), y_sample.reshape(DEC_BATCH, DEC_SEQ, D_MODEL),
            pool_p, pool_s, ckv_p, ckv_s, kpe_p, kpe_s, conv_p, conv_s, ssm_p, ssm_s)
```

```python
import functools

import jax
import jax.numpy as jnp
import numpy as np
from jax import lax
from jax.experimental import pallas as pl
from jax.experimental.pallas import tpu as pltpu

F32 = jnp.float32
BF16 = jnp.bfloat16

D_MODEL = 2048
BATCH = 1
SEQ = 16384
DEPTH = 2
DEC_BATCH = 16
DEC_SEQ = 64
PAST_LEN = 1024
CHUNK = 64
ALPHA = (2 * DEPTH) ** 0.25
LN_EPS = 1e-5
RMS_EPS = 1e-6
FF_DIM = 5504
POOL_WINDOWS = (2, 4, 8, 16)
POOL_DIM = 512
POOL_GROUP = 128
POOL_STATE = 15
NOPE_DIM = 128
ROPE_DIM = 64
V_DIM = 128
MLA_HEADS = 12
Q_LORA = 512
KV_LORA = 512
ROPE_THETA = 10000.0
SSM_INNER = 4096
SSM_HEAD_DIM = 64
SSM_HEADS = 64
SSM_GROUPS = 8
SSM_HPG = 8
SSM_STATE = 128
CONV_WIDTH = 4
CONV_DIM = SSM_INNER + 2 * SSM_GROUPS * SSM_STATE

NP = BATCH * SEQ
NS = DEC_BATCH * DEC_SEQ
NTOK = NP + NS

LANES = 128
SUBLANES = 8
VMEM_LIMIT_BYTES = 56 * 1024 * 1024

FF_TILE = 512
FF_MAIN_TILES = FF_DIM // FF_TILE
FF_TAIL = FF_DIM - FF_MAIN_TILES * FF_TILE
FFN_TM = 512
TOK_TM = 512
OUT_ROW_SPLIT = 2
ATT_TQ = 1024
ATT_TK = 512
SSD_L = 128
QK_PAD = 256
V_AUG = V_DIM + 2 * SUBLANES
SSD_IN_TM = 1024
SSD_IN_TN = 1024
POOL_EXT = 16
ATT_SCALE = float((NOPE_DIM + ROPE_DIM) ** -0.5)
ATT_SCALE_LOG2E = ATT_SCALE * 1.4426950408889634


def _cparams(sem):
    return pltpu.CompilerParams(dimension_semantics=sem, vmem_limit_bytes=VMEM_LIMIT_BYTES)


def _layer_norm(y, g, b):
    mu = jnp.mean(y, axis=-1, keepdims=True)
    d = y - mu
    var = jnp.mean(d * d, axis=-1, keepdims=True)
    return d * lax.rsqrt(var + LN_EPS) * g + b


def _rms(h, g):
    return h * lax.rsqrt(jnp.mean(h * h, axis=-1, keepdims=True) + RMS_EPS) * g


def _silu(x):
    h = 0.5 * x
    return h + h * jnp.tanh(h)


def _dot(a, b):
    return jnp.dot(a, b, preferred_element_type=F32)


def _dot_nt(a, b):
    return lax.dot_general(a, b, (((1,), (1,)), ((), ())), preferred_element_type=F32)


def _dot_tn(a, b):
    return lax.dot_general(a, b, (((0,), (0,)), ((), ())), preferred_element_type=F32)


def _split2(v):
    hi = v.astype(BF16)
    lo = (v - hi.astype(F32)).astype(BF16)
    return hi, lo


def _split3(v):
    hi = v.astype(BF16)
    r = v - hi.astype(F32)
    mid = r.astype(BF16)
    lo = (r - mid.astype(F32)).astype(BF16)
    return hi, mid, lo


def _ffn_kernel(npt, n_in, n_out, *refs):
    x_refs = refs[:n_in]
    wg_ref, wu_ref, wd_ref, wgt_ref, wut_ref, wdt_ref, g_ref, b_ref = refs[n_in:n_in + 8]
    o_refs = refs[n_in + 8:n_in + 8 + n_out]
    xb_ref = refs[n_in + 8 + n_out]
    acc_ref = o_refs[0] if n_out == 1 else refs[n_in + 9 + n_out]
    i = pl.program_id(0)
    j = pl.program_id(1)

    def load_x(rows=slice(None)):
        if n_in == 1:
            return x_refs[0][rows, :]
        return jnp.where(i < npt, x_refs[0][rows, :], x_refs[1][rows, :])

    def hidden(xb, wg, wu):
        return (_silu(_dot(xb, wg)) * _dot(xb, wu)).astype(BF16)

    last = pl.num_programs(1) - 1

    @pl.when(j == 0)
    def _():
        xb = load_x().astype(BF16)
        xb_ref[...] = xb
        acc_ref[...] = _dot(hidden(xb, wgt_ref[0], wut_ref[0]), wdt_ref[0])

    @pl.when(j < last)
    def _():
        acc_ref[...] += _dot(hidden(xb_ref[...], wg_ref[...], wu_ref[0]), wd_ref[...])

    @pl.when(j == last)
    def _():
        h = hidden(xb_ref[...], wg_ref[...], wu_ref[0])
        half = xb_ref.shape[0] // OUT_ROW_SPLIT
        for r in range(OUT_ROW_SPLIT):
            rows = pl.ds(r * half, half)
            acc = acc_ref[rows, :] + _dot(h[r * half:(r + 1) * half, :], wd_ref[...])
            y = _layer_norm(ALPHA * load_x(rows) + 0.5 * acc, g_ref[...], b_ref[...])
            if n_out == 1:
                o_refs[0][rows, :] = y
            else:
                @pl.when(i < npt)
                def _():
                    o_refs[0][rows, :] = y

                @pl.when(i >= npt)
                def _():
                    o_refs[1][rows, :] = y


def _ffn(xs, w_up_b, w_down_b, layer, g, b, split_out=False):
    tm = FFN_TM
    npt = NP // tm
    nff = FF_MAIN_TILES
    tail0 = nff * FF_TILE
    el = pl.Element
    aligned = lambda v: pl.multiple_of(v, LANES)
    n_in = len(xs)
    n_out = 2 if split_out else 1
    prompt_tile = lambda i, j: (jnp.minimum(i, npt - 1), 0)
    sample_tile = lambda i, j: (jnp.maximum(i - npt, 0), 0)
    whole_tile = lambda i, j: (i, 0)
    x_specs = [pl.BlockSpec((tm, D_MODEL), m) for m in ((whole_tile,) if n_in == 1 else (prompt_tile, sample_tile))]
    if split_out:
        out_specs = [pl.BlockSpec((tm, D_MODEL), prompt_tile), pl.BlockSpec((tm, D_MODEL), sample_tile)]
        out_shape = [jax.ShapeDtypeStruct((NP, D_MODEL), F32), jax.ShapeDtypeStruct((NS, D_MODEL), F32)]
        scratch = [pltpu.VMEM((tm, D_MODEL), BF16), pltpu.VMEM((tm, D_MODEL), F32)]
    else:
        out_specs = pl.BlockSpec((tm, D_MODEL), whole_tile)
        out_shape = jax.ShapeDtypeStruct((NTOK, D_MODEL), F32)
        scratch = [pltpu.VMEM((tm, D_MODEL), BF16)]
    return pl.pallas_call(
        functools.partial(_ffn_kernel, npt, n_in, n_out),
        grid=(NTOK // tm, nff),
        in_specs=x_specs + [
            pl.BlockSpec((None, D_MODEL, FF_TILE), lambda i, j: (layer, 0, j)),
            pl.BlockSpec((el(1), el(D_MODEL), el(FF_TILE)), lambda i, j: (layer, 0, aligned(FF_DIM + j * FF_TILE))),
            pl.BlockSpec((None, FF_TILE, D_MODEL), lambda i, j: (layer, j, 0)),
            pl.BlockSpec((el(1), el(D_MODEL), el(FF_TAIL)), lambda i, j: (layer, 0, tail0), pipeline_mode=pl.Buffered(1)),
            pl.BlockSpec((el(1), el(D_MODEL), el(FF_TAIL)), lambda i, j: (layer, 0, FF_DIM + tail0), pipeline_mode=pl.Buffered(1)),
            pl.BlockSpec((el(1), el(FF_TAIL), el(D_MODEL)), lambda i, j: (layer, tail0, 0), pipeline_mode=pl.Buffered(1)),
            pl.BlockSpec((1, D_MODEL), lambda i, j: (0, 0)),
            pl.BlockSpec((1, D_MODEL), lambda i, j: (0, 0)),
        ],
        out_specs=out_specs,
        out_shape=out_shape,
        scratch_shapes=scratch,
        compiler_params=_cparams(("arbitrary", "arbitrary")),
        name="ffn_postnorm",
    )(*xs, w_up_b, w_up_b, w_down_b, w_up_b, w_up_b, w_down_b, g.reshape(1, -1), b.reshape(1, -1))


def _pool_window_mean_minus(ext_ref, base, rows, pos0):
    first = base + POOL_EXT
    t = lax.broadcasted_iota(jnp.int32, (rows, POOL_GROUP), 0)
    posp1 = (pos0 + t + 1).astype(F32)
    outs = []
    for g, w in enumerate(POOL_WINDOWS):
        cols = slice(g * POOL_GROUP, (g + 1) * POOL_GROUP)
        cur = ext_ref[pl.ds(first, rows), cols]
        tot = cur
        for s in range(1, w):
            tot = tot + ext_ref[pl.ds(first - s, rows), cols]
        outs.append(tot / jnp.minimum(posp1, float(w)) - cur)
    return outs


def _ab_in_kernel(n_prompt_tiles, x_ref, w_ref, past_ref, pw_ref, ps_ref, qg_ref, kg_ref, rope_ref,
                  ut_ref, yp_ref, qn_ref, ckvp_ref, ckvs_ref, kpep_ref, kpes_ref, ext_ref):
    i = pl.program_id(0)
    tm = x_ref.shape[0]
    h = _dot(x_ref[...].astype(BF16), w_ref[...])
    u = h[:, :POOL_DIM]
    for s in range(tm // DEC_SEQ):
        ut_ref[s] = u[(s + 1) * DEC_SEQ - POOL_EXT:(s + 1) * DEC_SEQ, :]
    qn_ref[...] = _rms(h[:, POOL_DIM:POOL_DIM + Q_LORA], qg_ref[...]).astype(BF16)
    ckv = _rms(h[:, POOL_DIM + Q_LORA:POOL_DIM + Q_LORA + KV_LORA], kg_ref[...])
    kv = h[:, POOL_DIM + Q_LORA + KV_LORA:] * rope_ref[...]
    kpe = (kv + pltpu.roll(kv, ROPE_DIM, axis=1))[:, :ROPE_DIM]

    @pl.when(i < n_prompt_tiles)
    def _():
        ckvp_ref[...] = ckv
        kpep_ref[...] = kpe

    @pl.when(i >= n_prompt_tiles)
    def _():
        ckvs_ref[...] = ckv
        kpes_ref[...] = kpe

    def finish(d_groups):
        for g in range(len(POOL_WINDOWS)):
            cols = slice(g * POOL_GROUP, (g + 1) * POOL_GROUP)
            y = _dot(d_groups[g].astype(BF16), pw_ref[g]) * ps_ref[:, cols]
            yp_ref[:, cols] = y.astype(BF16)

    @pl.when(i < n_prompt_tiles)
    def _():
        @pl.when(i == 0)
        def _():
            ext_ref[pl.ds(0, POOL_EXT), :] = jnp.zeros((POOL_EXT, POOL_DIM), F32)

        ext_ref[pl.ds(POOL_EXT, tm), :] = u
        finish(_pool_window_mean_minus(ext_ref, 0, tm, i * tm))
        ext_ref[pl.ds(0, POOL_EXT), :] = u[tm - POOL_EXT:, :]

    @pl.when(i >= n_prompt_tiles)
    def _():
        nseg = tm // DEC_SEQ
        stride = POOL_EXT + DEC_SEQ
        parts = [[] for _ in POOL_WINDOWS]
        for s in range(nseg):
            ext_ref[pl.ds(s * stride, POOL_EXT), :] = past_ref[s]
            ext_ref[pl.ds(s * stride + POOL_EXT, DEC_SEQ), :] = u[s * DEC_SEQ:(s + 1) * DEC_SEQ, :]
        for s in range(nseg):
            d = _pool_window_mean_minus(ext_ref, s * stride, DEC_SEQ, PAST_LEN)
            for g in range(len(POOL_WINDOWS)):
                parts[g].append(d[g])
        finish([jnp.concatenate(p, axis=0) for p in parts])


def _ab_in(x, w_in_x, pool_past, pool_w, pool_scale, q_norm_g, kv_norm_g, rope_k):
    n = x.shape[0]
    tm = TOK_TM
    nseg = tm // DEC_SEQ
    npt = NP // tm
    wcols = w_in_x.shape[1]
    ext_rows = max(tm + POOL_EXT, nseg * (POOL_EXT + DEC_SEQ))
    const = lambda i: (0, 0)
    prompt_tile = lambda i: (jnp.minimum(i, npt - 1), 0)
    sample_tile = lambda i: (jnp.maximum(i - npt, 0), 0)
    return pl.pallas_call(
        functools.partial(_ab_in_kernel, npt),
        grid=(n // tm,),
        in_specs=[
            pl.BlockSpec((tm, D_MODEL), lambda i: (i, 0)),
            pl.BlockSpec((D_MODEL, wcols), const),
            pl.BlockSpec((nseg, POOL_EXT, POOL_DIM), lambda i: (jnp.maximum(i - npt, 0), 0, 0)),
            pl.BlockSpec((len(POOL_WINDOWS), POOL_GROUP, POOL_GROUP), lambda i: (0, 0, 0)),
            pl.BlockSpec((1, POOL_DIM), const),
            pl.BlockSpec((1, Q_LORA), const),
            pl.BlockSpec((1, KV_LORA), const),
            pl.BlockSpec((tm, LANES), lambda i: (i, 0)),
        ],
        out_specs=[
            pl.BlockSpec((nseg, POOL_EXT, POOL_DIM), lambda i: (i, 0, 0)),
            pl.BlockSpec((tm, POOL_DIM), lambda i: (i, 0)),
            pl.BlockSpec((tm, Q_LORA), lambda i: (i, 0)),
            pl.BlockSpec((tm, KV_LORA), prompt_tile),
            pl.BlockSpec((tm, KV_LORA), sample_tile),
            pl.BlockSpec((tm, ROPE_DIM), prompt_tile),
            pl.BlockSpec((tm, ROPE_DIM), sample_tile),
        ],
        out_shape=[
            jax.ShapeDtypeStruct((n // DEC_SEQ, POOL_EXT, POOL_DIM), F32),
            jax.ShapeDtypeStruct((n, POOL_DIM), BF16),
            jax.ShapeDtypeStruct((n, Q_LORA), BF16),
            jax.ShapeDtypeStruct((NP, KV_LORA), F32),
            jax.ShapeDtypeStruct((NS, KV_LORA), F32),
            jax.ShapeDtypeStruct((NP, ROPE_DIM), F32),
            jax.ShapeDtypeStruct((NS, ROPE_DIM), F32),
        ],
        scratch_shapes=[pltpu.VMEM((ext_rows, POOL_DIM), F32)],
        compiler_params=_cparams(("arbitrary",)),
        name="ab_in",
    )(x, w_in_x, pool_past, pool_w.astype(BF16), pool_scale.reshape(1, -1),
      q_norm_g.reshape(1, -1), kv_norm_g.reshape(1, -1), rope_k)


def _q_up_kernel(qn_ref, w_ref, rope_ref, q_ref):
    qn = qn_ref[...]
    rope = rope_ref[...]
    lane = lax.broadcasted_iota(jnp.int32, rope.shape, 1)
    hpc = 4
    ncol = hpc * LANES
    for c in range(MLA_HEADS // hpc):
        nope = _dot(qn, w_ref[:, c * ncol:(c + 1) * ncol])
        rp = _dot(qn, w_ref[:, MLA_HEADS * LANES + c * ncol:MLA_HEADS * LANES + (c + 1) * ncol])
        for hh in range(hpc):
            hd = c * hpc + hh
            v = rp[:, hh * LANES:(hh + 1) * LANES] * rope
            rot = jnp.where(lane < ROPE_DIM, v + pltpu.roll(v, ROPE_DIM, axis=1), 0.0)
            q_ref[hd, :, 0:LANES] = (nope[:, hh * LANES:(hh + 1) * LANES] * ATT_SCALE).astype(BF16)
            q_ref[hd, :, LANES:QK_PAD] = (rot * ATT_SCALE).astype(BF16)


def _q_up(qn, wq_x, rope_q, row0, nrows):
    tm = TOK_TM
    t0 = row0 // tm
    return pl.pallas_call(
        _q_up_kernel,
        grid=(nrows // tm,),
        in_specs=[
            pl.BlockSpec((tm, Q_LORA), lambda i: (t0 + i, 0)),
            pl.BlockSpec(wq_x.shape, lambda i: (0, 0)),
            pl.BlockSpec((tm, LANES), lambda i: (t0 + i, 0)),
        ],
        out_specs=pl.BlockSpec((MLA_HEADS, tm, QK_PAD), lambda i: (0, i, 0)),
        out_shape=jax.ShapeDtypeStruct((MLA_HEADS, nrows, QK_PAD), BF16),
        compiler_params=_cparams(("parallel",)),
        name="q_up",
    )(qn, wq_x, rope_q)


def _q_up_t_kernel(qn_ref, wt_ref, ropet_ref, qt_ref):
    qt_all = _dot_nt(wt_ref[...], qn_ref[...])
    ropet = ropet_ref[...]
    nrope = MLA_HEADS * LANES
    zeros = jnp.zeros((QK_PAD - LANES - ROPE_DIM, qt_all.shape[1]), BF16)
    for hd in range(MLA_HEADS):
        v = qt_all[nrope + hd * LANES:nrope + (hd + 1) * LANES, :] * ropet
        rot = v[:ROPE_DIM] + v[ROPE_DIM:]
        qt_ref[hd, 0, 0:LANES, :] = (qt_all[hd * LANES:(hd + 1) * LANES, :] * ATT_SCALE_LOG2E).astype(BF16)
        qt_ref[hd, 0, LANES:LANES + ROPE_DIM, :] = (rot * ATT_SCALE_LOG2E).astype(BF16)
        qt_ref[hd, 0, LANES + ROPE_DIM:QK_PAD, :] = zeros


def _q_up_t(qn, wq_xt, rope_t, nrows):
    tm = ATT_TQ
    return pl.pallas_call(
        _q_up_t_kernel,
        grid=(nrows // tm,),
        in_specs=[
            pl.BlockSpec((tm, Q_LORA), lambda i: (i, 0)),
            pl.BlockSpec(wq_xt.shape, lambda i: (0, 0)),
            pl.BlockSpec((LANES, tm), lambda i: (0, i)),
        ],
        out_specs=pl.BlockSpec((MLA_HEADS, 1, QK_PAD, tm), lambda i: (0, i, 0, 0)),
        out_shape=jax.ShapeDtypeStruct((MLA_HEADS, nrows // tm, QK_PAD, tm), BF16),
        compiler_params=_cparams(("parallel",)),
        name="q_up_t",
    )(qn, wq_xt, rope_t)


def _kv_up_kernel(ckv_ref, kpe_ref, w_ref, k_ref, v_ref):
    c = ckv_ref[...].astype(BF16)
    kpe = kpe_ref[...].astype(BF16)
    hpc = 4
    ncol = hpc * LANES
    for j in range(MLA_HEADS // hpc):
        kn = _dot(c, w_ref[:, j * ncol:(j + 1) * ncol])
        vv = _dot(c, w_ref[:, MLA_HEADS * LANES + j * ncol:MLA_HEADS * LANES + (j + 1) * ncol])
        for hh in range(hpc):
            hd = j * hpc + hh
            k_ref[hd, :, 0:LANES] = kn[:, hh * LANES:(hh + 1) * LANES].astype(BF16)
            k_ref[hd, :, LANES:LANES + ROPE_DIM] = kpe
            k_ref[hd, :, LANES + ROPE_DIM:QK_PAD] = jnp.zeros_like(kpe)
            v_ref[hd] = vv[:, hh * LANES:(hh + 1) * LANES].astype(BF16)


def _kv_up(ckv, kpe, wkv_x):
    tm = TOK_TM
    nrows = ckv.shape[0]
    return pl.pallas_call(
        _kv_up_kernel,
        grid=(nrows // tm,),
        in_specs=[
            pl.BlockSpec((tm, KV_LORA), lambda i: (i, 0)),
            pl.BlockSpec((tm, ROPE_DIM), lambda i: (i, 0)),
            pl.BlockSpec(wkv_x.shape, lambda i: (0, 0)),
        ],
        out_specs=[
            pl.BlockSpec((MLA_HEADS, tm, QK_PAD), lambda i: (0, i, 0)),
            pl.BlockSpec((MLA_HEADS, tm, V_DIM), lambda i: (0, i, 0)),
        ],
        out_shape=[
            jax.ShapeDtypeStruct((MLA_HEADS, nrows, QK_PAD), BF16),
            jax.ShapeDtypeStruct((MLA_HEADS, nrows, V_DIM), BF16),
        ],
        compiler_params=_cparams(("parallel",)),
        name="kv_up",
    )(ckv, kpe, wkv_x)


def _kv_up_t_kernel(ckv_ref, kpe_ref, wk_ref, wvt_ref, k_ref, vt_ref):
    c = ckv_ref[...].astype(BF16)
    kpe = kpe_ref[...].astype(BF16)
    hpc = 4
    ncol = hpc * LANES
    for j in range(MLA_HEADS // hpc):
        kn = _dot(c, wk_ref[:, j * ncol:(j + 1) * ncol])
        for hh in range(hpc):
            hd = j * hpc + hh
            k_ref[hd, :, 0:LANES] = kn[:, hh * LANES:(hh + 1) * LANES].astype(BF16)
            k_ref[hd, :, LANES:LANES + ROPE_DIM] = kpe
            k_ref[hd, :, LANES + ROPE_DIM:QK_PAD] = jnp.zeros_like(kpe)
    vt_all = _dot_nt(wvt_ref[...], c)
    ones = jnp.ones((V_AUG - V_DIM, vt_all.shape[1]), BF16)
    for hd in range(MLA_HEADS):
        vt_ref[hd, 0, 0:V_DIM, :] = vt_all[hd * V_DIM:(hd + 1) * V_DIM, :].astype(BF16)
        vt_ref[hd, 0, V_DIM:V_AUG, :] = ones


def _kv_up_t(ckv, kpe, wk, wvt):
    tm = ATT_TK
    nrows = ckv.shape[0]
    return pl.pallas_call(
        _kv_up_t_kernel,
        grid=(nrows // tm,),
        in_specs=[
            pl.BlockSpec((tm, KV_LORA), lambda i: (i, 0)),
            pl.BlockSpec((tm, ROPE_DIM), lambda i: (i, 0)),
            pl.BlockSpec(wk.shape, lambda i: (0, 0)),
            pl.BlockSpec(wvt.shape, lambda i: (0, 0)),
        ],
        out_specs=[
            pl.BlockSpec((MLA_HEADS, tm, QK_PAD), lambda i: (0, i, 0)),
            pl.BlockSpec((MLA_HEADS, 1, V_AUG, tm), lambda i: (0, i, 0, 0)),
        ],
        out_shape=[
            jax.ShapeDtypeStruct((MLA_HEADS, nrows, QK_PAD), BF16),
            jax.ShapeDtypeStruct((MLA_HEADS, nrows // tm, V_AUG, tm), BF16),
        ],
        compiler_params=_cparams(("parallel",)),
        name="kv_up_t",
    )(ckv, kpe, wk, wvt)


def _attn_prompt_kernel(qt_ref, k_ref, vt_ref, o_ref, s0_ref, s1_ref, m_ref, acc_ref):
    i = pl.program_id(1)
    tk = s0_ref.shape[0]
    tq = qt_ref.shape[3]

    def scores(c, s_ref, lanes=slice(None)):
        start = pl.multiple_of(c * tk, tk)
        s_ref[:, lanes] = _dot(k_ref[0, pl.ds(start, tk), :], qt_ref[0, 0, :, lanes])

    def softmax_pv(c, s_ref, mask, lanes=slice(None)):
        st = s_ref[:, lanes]
        if mask is not None:
            st = jnp.where(mask, st, -jnp.inf)
        m = m_ref[:, lanes]
        m_new = jnp.maximum(m, jnp.max(st, axis=0, keepdims=True))
        alpha = jnp.exp2(m - m_new)
        p = jnp.exp2(st - m_new)
        m_ref[:, lanes] = m_new
        acc_ref[:, lanes] = alpha * acc_ref[:, lanes] + _dot(vt_ref[0, c], p.astype(BF16))

    m_ref[...] = jnp.full(m_ref.shape, -jnp.inf, F32)
    acc_ref[...] = jnp.zeros(acc_ref.shape, F32)
    key_chunk = lax.broadcasted_iota(jnp.int32, (tk, tq), 0) // CHUNK
    query_chunk = lax.broadcasted_iota(jnp.int32, (tk, tq), 1) // CHUNK
    diag = query_chunk >= key_chunk

    scores(0, s0_ref)

    def pair(j, carry):
        scores(2 * j + 1, s1_ref)
        softmax_pv(2 * j, s0_ref, None)
        scores(2 * j + 2, s0_ref)
        softmax_pv(2 * j + 1, s1_ref, None)
        return carry

    lax.fori_loop(0, i, pair, 0)
    upper = slice(tk, tq)
    scores(2 * i + 1, s1_ref, upper)
    softmax_pv(2 * i, s0_ref, diag)
    softmax_pv(2 * i + 1, s1_ref, diag[:, 0:tk], upper)

    o_ref[...] = (acc_ref[0:V_DIM, :] / acc_ref[V_DIM:V_DIM + 1, :]).T.astype(BF16)


def _attn_prompt(qt, k, vt):
    tq, tk = ATT_TQ, ATT_TK
    assert tq == 2 * tk and tk % CHUNK == 0
    score = pltpu.VMEM((tk, tq), F32)
    return pl.pallas_call(
        _attn_prompt_kernel,
        grid=(MLA_HEADS, NP // tq),
        in_specs=[
            pl.BlockSpec((1, 1, QK_PAD, tq), lambda h, i: (h, i, 0, 0)),
            pl.BlockSpec((1, NP, QK_PAD), lambda h, i: (h, 0, 0)),
            pl.BlockSpec((1, NP // tk, V_AUG, tk), lambda h, i: (h, 0, 0, 0)),
        ],
        out_specs=pl.BlockSpec((tq, V_DIM), lambda h, i: (i, h)),
        out_shape=jax.ShapeDtypeStruct((NP, MLA_HEADS * V_DIM), BF16),
        scratch_shapes=[score, score, pltpu.VMEM((1, tq), F32), pltpu.VMEM((V_AUG, tq), F32)],
        compiler_params=_cparams(("parallel", "arbitrary")),
        name="attn_prompt",
    )(qt, k, vt)


def _attn_sample_kernel(q_ref, kp_ref, vp_ref, kn_ref, vn_ref, o_ref):
    for hd in range(MLA_HEADS):
        q = q_ref[hd]
        sp = _dot_nt(q, kp_ref[hd])
        sn = _dot_nt(q, kn_ref[hd])
        m = jnp.maximum(jnp.max(sp, axis=1, keepdims=True), jnp.max(sn, axis=1, keepdims=True))
        pp = jnp.exp(sp - m)
        pn = jnp.exp(sn - m)
        l = jnp.sum(pp, axis=1, keepdims=True) + jnp.sum(pn, axis=1, keepdims=True)
        o = _dot(pp.astype(BF16), vp_ref[hd]) + _dot(pn.astype(BF16), vn_ref[hd])
        o_ref[:, hd * V_DIM:(hd + 1) * V_DIM] = (o / l).astype(BF16)


def _attn_sample(q, k_past, v_past, k_new, v_new):
    assert PAST_LEN % CHUNK == 0 and DEC_SEQ <= CHUNK
    return pl.pallas_call(
        _attn_sample_kernel,
        grid=(DEC_BATCH,),
        in_specs=[
            pl.BlockSpec((MLA_HEADS, DEC_SEQ, QK_PAD), lambda b: (0, b, 0)),
            pl.BlockSpec((MLA_HEADS, PAST_LEN, QK_PAD), lambda b: (0, b, 0)),
            pl.BlockSpec((MLA_HEADS, PAST_LEN, V_DIM), lambda b: (0, b, 0)),
            pl.BlockSpec((MLA_HEADS, DEC_SEQ, QK_PAD), lambda b: (0, b, 0)),
            pl.BlockSpec((MLA_HEADS, DEC_SEQ, V_DIM), lambda b: (0, b, 0)),
        ],
        out_specs=pl.BlockSpec((DEC_SEQ, MLA_HEADS * V_DIM), lambda b: (b, 0)),
        out_shape=jax.ShapeDtypeStruct((NS, MLA_HEADS * V_DIM), BF16),
        compiler_params=_cparams(("parallel",)),
        name="attn_sample",
    )(q, k_past, v_past, k_new, v_new)


def _ab_out_kernel(npt, x_ref, yp_ref, ap_ref, as_ref, wp_ref, wa_ref, g_ref, b_ref, o_ref):
    i = pl.program_id(0)
    half = x_ref.shape[0] // OUT_ROW_SPLIT
    for r in range(OUT_ROW_SPLIT):
        rows = pl.ds(r * half, half)
        att = jnp.where(i < npt, ap_ref[rows, :], as_ref[rows, :])
        mix = _dot(yp_ref[rows, :], wp_ref[...]) + _dot(att, wa_ref[...])
        o_ref[rows, :] = _layer_norm(ALPHA * x_ref[rows, :] + mix, g_ref[...], b_ref[...])


def _ab_out(x, ypool, att_p, att_s, w_out, g, b):
    n = x.shape[0]
    tm = TOK_TM
    npt = NP // tm
    w = w_out.astype(BF16)
    adim = MLA_HEADS * V_DIM
    const = lambda i: (0, 0)
    return pl.pallas_call(
        functools.partial(_ab_out_kernel, npt),
        grid=(n // tm,),
        in_specs=[
            pl.BlockSpec((tm, D_MODEL), lambda i: (i, 0)),
            pl.BlockSpec((tm, POOL_DIM), lambda i: (i, 0)),
            pl.BlockSpec((tm, adim), lambda i: (jnp.minimum(i, npt - 1), 0)),
            pl.BlockSpec((tm, adim), lambda i: (jnp.maximum(i - npt, 0), 0)),
            pl.BlockSpec((POOL_DIM, D_MODEL), const),
            pl.BlockSpec((adim, D_MODEL), const),
            pl.BlockSpec((1, D_MODEL), const),
            pl.BlockSpec((1, D_MODEL), const),
        ],
        out_specs=pl.BlockSpec((tm, D_MODEL), lambda i: (i, 0)),
        out_shape=jax.ShapeDtypeStruct((n, D_MODEL), F32),
        compiler_params=_cparams(("parallel",)),
        name="ab_out",
    )(x, ypool, att_p, att_s, w[:POOL_DIM], w[POOL_DIM:], g.reshape(1, -1), b.reshape(1, -1))


def _ssd_out_kernel(npt, x_ref, ap_ref, as_ref, w_ref, g_ref, b_ref, o_ref):
    i = pl.program_id(0)
    half = x_ref.shape[0] // OUT_ROW_SPLIT
    for r in range(OUT_ROW_SPLIT):
        rows = pl.ds(r * half, half)
        act = jnp.where(i < npt, ap_ref[rows, :], as_ref[rows, :])
        o_ref[rows, :] = _layer_norm(ALPHA * x_ref[rows, :] + _dot(act, w_ref[...]), g_ref[...], b_ref[...])


def _ssd_out(x, act_p, act_s, w_out, g, b):
    n = x.shape[0]
    tm = TOK_TM
    npt = NP // tm
    const = lambda i: (0, 0)
    return pl.pallas_call(
        functools.partial(_ssd_out_kernel, npt),
        grid=(n // tm,),
        in_specs=[
            pl.BlockSpec((tm, D_MODEL), lambda i: (i, 0)),
            pl.BlockSpec((tm, SSM_INNER), lambda i: (jnp.minimum(i, npt - 1), 0)),
            pl.BlockSpec((tm, SSM_INNER), lambda i: (jnp.maximum(i - npt, 0), 0)),
            pl.BlockSpec((SSM_INNER, D_MODEL), const),
            pl.BlockSpec((1, D_MODEL), const),
            pl.BlockSpec((1, D_MODEL), const),
        ],
        out_specs=pl.BlockSpec((tm, D_MODEL), lambda i: (i, 0)),
        out_shape=jax.ShapeDtypeStruct((n, D_MODEL), F32),
        compiler_params=_cparams(("parallel",)),
        name="ssd_out",
    )(x, act_p, act_s, w_out.astype(BF16), g.reshape(1, -1), b.reshape(1, -1))


def _ssd_in_kernel(x_ref, w_ref, wdt_ref, o_ref, dt_ref, xb_ref):
    @pl.when(pl.program_id(1) == 0)
    def _():
        xb_ref[...] = x_ref[...].astype(BF16)
        dt_ref[...] = _dot(xb_ref[...], wdt_ref[...])

    o_ref[...] = _dot(xb_ref[...], w_ref[...])


def _ssd_in(x, w_all, w_dt):
    n = x.shape[0]
    tm = SSD_IN_TM
    ncol = SSM_INNER + CONV_DIM
    w_zx = w_all
    return pl.pallas_call(
        _ssd_in_kernel,
        grid=(n // tm, ncol // SSD_IN_TN),
        in_specs=[
            pl.BlockSpec((tm, D_MODEL), lambda i, j: (i, 0)),
            pl.BlockSpec((D_MODEL, SSD_IN_TN), lambda i, j: (0, j)),
            pl.BlockSpec((D_MODEL, LANES), lambda i, j: (0, 0)),
        ],
        out_specs=[
            pl.BlockSpec((tm, SSD_IN_TN), lambda i, j: (i, j)),
            pl.BlockSpec((tm, LANES), lambda i, j: (i, 0)),
        ],
        out_shape=[
            jax.ShapeDtypeStruct((n, ncol), F32),
            jax.ShapeDtypeStruct((n, LANES), F32),
        ],
        scratch_shapes=[pltpu.VMEM((tm, D_MODEL), BF16)],
        compiler_params=_cparams(("parallel", "arbitrary")),
        name="ssd_in",
    )(x, w_zx, w_dt)


def _ssd_kernel(rows, fresh_each_step, z_ref, xs_ref, bc_ref, dt_ref, cpast_ref, h0_ref, cw_ref,
                cb_ref, dtb_ref, a_ref, dsk_ref, ng_ref, e_ref, y_ref, h_ref, ext_ref):
    c = pl.program_id(0)
    L = SSD_L
    G, R, P, S = SSM_GROUPS, SSM_HPG, SSM_HEAD_DIM, SSM_STATE
    halo = SUBLANES

    def start_sequence():
        ext_ref[pl.ds(0, halo), :] = cpast_ref[0]
        h_ref[0] = h0_ref[0]

    if fresh_each_step:
        start_sequence()
    else:
        pl.when(c == 0)(start_sequence)

    ext_ref[pl.ds(halo, rows), 0:SSM_INNER] = xs_ref[...]
    ext_ref[pl.ds(halo, rows), SSM_INNER:CONV_DIM] = bc_ref[...]
    if rows < L:
        ext_ref[pl.ds(halo + rows, L - rows), :] = jnp.zeros((L - rows, CONV_DIM), F32)

    assert CONV_WIDTH == 4
    win = ext_ref[pl.ds(0, halo + L), :]
    prev = pltpu.roll(win, 1, axis=0)
    near = cw_ref[3:4, :] * win[halo:] + cw_ref[2:3, :] * prev[halo:]
    far = cw_ref[1:2, :] * win + cw_ref[0:1, :] * prev
    conv = cb_ref[...] + near + pltpu.roll(far, 2, axis=0)[halo:]
    ext_ref[pl.ds(0, halo), :] = ext_ref[pl.ds(rows, halo), :]
    act = _silu(conv)
    xs = act[:, :SSM_INNER]
    xsb = xs.astype(BF16)
    bmat = act[:, SSM_INNER:SSM_INNER + G * S].astype(BF16)
    cmat = act[:, SSM_INNER + G * S:].astype(BF16)

    dtr = dt_ref[...] + dtb_ref[...]
    dt = jnp.maximum(dtr, 0.0) + jnp.log1p(jnp.exp(-jnp.abs(dtr)))
    if rows < L:
        dt = jnp.concatenate([dt, jnp.zeros((L - rows, LANES), F32)], axis=0)
    da = dt * a_ref[...]
    ti = lax.broadcasted_iota(jnp.int32, (L, L), 0)
    si = lax.broadcasted_iota(jnp.int32, (L, L), 1)
    causal = ti >= si
    tri = jnp.where(causal, 1.0, 0.0).astype(BF16)
    acs = sum(_dot(tri, part) for part in _split3(da))
    acs_t = acs.T
    dt_t = dt.T
    last = acs[L - 1:L, :]
    to_end = jnp.exp(last - acs) * dt
    eacs = jnp.exp(acs)
    onehot = e_ref[...]
    expand = lambda v: sum(_dot(part, onehot) for part in _split2(v))
    x_end = (xs * expand(to_end)).astype(BF16)
    eacs_x = expand(eacs)
    dec_x = sum(_dot(part, onehot) for part in _split3(jnp.broadcast_to(jnp.exp(last), (SUBLANES, LANES))))[0:1, :]

    lane4 = lax.broadcasted_iota(jnp.int32, (L, 4 * P), 1) // P
    for g in range(G):
        bg = bmat[:, g * S:(g + 1) * S]
        cg = cmat[:, g * S:(g + 1) * S]
        cbm = _dot_nt(cg, bg)
        cols = slice(g * R * P, (g + 1) * R * P)
        ht = h_ref[0, :, cols]
        y_state = _dot(cg, ht.astype(BF16)) * eacs_x[:, cols]
        y_parts = []
        for half in range(R // 4):
            x4 = xsb[:, g * R * P + half * 4 * P:g * R * P + (half + 1) * 4 * P]
            x_bd = jnp.concatenate([jnp.where(lane4 == r, x4, jnp.zeros_like(x4)) for r in range(4)], axis=0)
            wts = []
            for r in range(4):
                hd = g * R + half * 4 + r
                seg = acs[:, hd:hd + 1] - acs_t[hd:hd + 1, :]
                decay = jnp.exp(jnp.where(causal, seg, -jnp.inf))
                wts.append((cbm * decay * dt_t[hd:hd + 1, :]).astype(BF16))
            y_parts.append(_dot(jnp.concatenate(wts, axis=1), x_bd))
        y = jnp.concatenate(y_parts, axis=1) + y_state + dsk_ref[:, cols] * xs[:, cols]
        gz = y[:rows] * _silu(z_ref[:, cols])
        gz = gz * lax.rsqrt(jnp.mean(gz * gz, axis=-1, keepdims=True) + RMS_EPS)
        y_ref[:, cols] = (gz * ng_ref[:, cols]).astype(BF16)
        h_ref[0, :, cols] = ht * dec_x[:, cols] + _dot_tn(bg, x_end[:, cols])


def _ssd_core(zx, dtx, row0, nblk, rows, fresh_each_step, conv_past8, h0_t, conv_w8, conv_b, dt_bias, a_neg,
              d_skip_x, norm_g, onehot):
    L = SSD_L
    r0 = row0 // rows
    nseq = nblk if fresh_each_step else 1
    seq = (lambda c: c) if fresh_each_step else (lambda c: 0)
    const = lambda c: (0, 0)
    return pl.pallas_call(
        functools.partial(_ssd_kernel, rows, fresh_each_step),
        grid=(nblk,),
        in_specs=[
            pl.BlockSpec((rows, SSM_INNER), lambda c: (r0 + c, 0)),
            pl.BlockSpec((rows, SSM_INNER), lambda c: (r0 + c, 1)),
            pl.BlockSpec((rows, 2 * SSM_GROUPS * SSM_STATE), lambda c: (r0 + c, 4)),
            pl.BlockSpec((rows, LANES), lambda c: (r0 + c, 0)),
            pl.BlockSpec((1, SUBLANES, CONV_DIM), lambda c: (seq(c), 0, 0)),
            pl.BlockSpec((1, SSM_STATE, SSM_INNER), lambda c: (seq(c), 0, 0)),
            pl.BlockSpec((SUBLANES, CONV_DIM), const),
            pl.BlockSpec((1, CONV_DIM), const),
            pl.BlockSpec((1, LANES), const),
            pl.BlockSpec((1, LANES), const),
            pl.BlockSpec((1, SSM_INNER), const),
            pl.BlockSpec((1, SSM_INNER), const),
            pl.BlockSpec((LANES, SSM_INNER), const),
        ],
        out_specs=[
            pl.BlockSpec((rows, SSM_INNER), lambda c: (c, 0)),
            pl.BlockSpec((1, SSM_STATE, SSM_INNER), lambda c: (seq(c), 0, 0)),
        ],
        out_shape=[
            jax.ShapeDtypeStruct((nblk * rows, SSM_INNER), BF16),
            jax.ShapeDtypeStruct((nseq, SSM_STATE, SSM_INNER), F32),
        ],
        scratch_shapes=[pltpu.VMEM((L + 2 * SUBLANES, CONV_DIM), F32)],
        compiler_params=_cparams(("arbitrary",)),
        name="ssd_core_seq" if not fresh_each_step else "ssd_core_blocks",
    )(zx, zx, zx, dtx, conv_past8, h0_t, conv_w8, conv_b, dt_bias, a_neg, d_skip_x, norm_g, onehot)


def _rope_table():
    half = ROPE_DIM // 2
    inv = (1.0 / (np.float32(ROPE_THETA) ** (np.arange(half, dtype=np.float32) * np.float32(2.0) / np.float32(ROPE_DIM)))).astype(np.float32)
    pos = np.concatenate([
        np.tile(np.arange(SEQ, dtype=np.float32), BATCH),
        np.tile(np.arange(DEC_SEQ, dtype=np.float32) + np.float32(PAST_LEN), DEC_BATCH)])
    ang = (pos[:, None] * inv[None, :]).astype(np.float32).astype(np.float64)
    cos, sin = np.cos(ang), np.sin(ang)
    return np.concatenate([cos, cos, -sin, sin], axis=1).astype(np.float32)


def _swap_halves(w):
    half = w.shape[-1] // 2
    return jnp.concatenate([w[..., half:], w[..., :half]], axis=-1)


def _prep_ab_weights(w_in_ab, w_q_up, w_kv_up):
    kpe_w = w_in_ab[:, POOL_DIM + Q_LORA + KV_LORA:]
    w_in_x = jnp.concatenate([w_in_ab, _swap_halves(kpe_w)], axis=1).astype(BF16)
    wq = w_q_up.reshape(Q_LORA, MLA_HEADS, NOPE_DIM + ROPE_DIM)
    wq_rope = wq[..., NOPE_DIM:]
    wq_x = jnp.concatenate([
        wq[..., :NOPE_DIM].reshape(Q_LORA, -1),
        jnp.concatenate([wq_rope, _swap_halves(wq_rope)], axis=-1).reshape(Q_LORA, -1)], axis=1).astype(BF16)
    wkv = w_kv_up.reshape(KV_LORA, MLA_HEADS, NOPE_DIM + V_DIM)
    wk = wkv[..., :NOPE_DIM].reshape(KV_LORA, -1).astype(BF16)
    wv = wkv[..., NOPE_DIM:].reshape(KV_LORA, -1).astype(BF16)
    return w_in_x, wq_x, wk, wv


def _state_to_t(h):
    b = h.shape[0]
    return jnp.transpose(h, (0, 3, 1, 2)).reshape(b, SSM_STATE, SSM_INNER)


def _state_from_t(ht):
    b = ht.shape[0]
    return jnp.transpose(ht.reshape(b, SSM_STATE, SSM_HEADS, SSM_HEAD_DIM), (0, 2, 3, 1))


def kernel(x_prompt, x_sample, state_pool, cache_ckv, cache_kpe, state_conv, state_ssm, ffn_pre_up, ffn_pre_down, ffn_post_up, ffn_post_down, ln_g, ln_b, w_in_ab, pool_w, pool_scale, q_norm_g, w_q_up, kv_norm_g, w_kv_up, w_out_ab, w_in_ssd, conv_w, conv_b, dt_bias, a_log, d_skip, ssm_norm_g, w_out_ssd):
    pre_up, pre_down = ffn_pre_up.astype(BF16), ffn_pre_down.astype(BF16)
    post_up, post_down = ffn_post_up.astype(BF16), ffn_post_down.astype(BF16)

    x = _ffn((x_prompt.reshape(NP, D_MODEL), x_sample.reshape(NS, D_MODEL)), pre_up, pre_down, 0,
             ln_g[0, 0], ln_b[0, 0])
    rope = _rope_table()
    w_in_x, wq_x, wk, wv = _prep_ab_weights(w_in_ab, w_q_up, w_kv_up)
    wkv_x = jnp.concatenate([wk, wv], axis=1)
    pool_past = jnp.pad(state_pool, ((0, 0), (POOL_EXT - POOL_STATE, 0), (0, 0)))
    utail, ypool, qn, ckv_p, ckv_s, kpe_p, kpe_s = _ab_in(
        x, w_in_x, pool_past, pool_w, pool_scale, q_norm_g, kv_norm_g, rope)
    qt_p = _q_up_t(qn, wq_x.T, np.ascontiguousarray(rope[:NP].T), NP)
    k_p, vt_p = _kv_up_t(ckv_p, kpe_p, wk, wv.T)
    q_s = _q_up(qn, wq_x, rope, NP, NS)
    k_s, v_s = _kv_up(ckv_s, kpe_s, wkv_x)
    k_past, v_past = _kv_up(cache_ckv.reshape(DEC_BATCH * PAST_LEN, KV_LORA),
                            cache_kpe.reshape(DEC_BATCH * PAST_LEN, ROPE_DIM), wkv_x)
    att_p = _attn_prompt(qt_p, k_p, vt_p)
    att_s = _attn_sample(q_s, k_past, v_past, k_s, v_s)
    x = _ab_out(x, ypool, att_p, att_s, w_out_ab, ln_g[0, 1], ln_b[0, 1])
    x = _ffn((x,), post_up, post_down, 0, ln_g[0, 2], ln_b[0, 2])

    skip = POOL_EXT - POOL_STATE
    pool_p = utail[:NP // DEC_SEQ].reshape(BATCH, SEQ // DEC_SEQ, POOL_EXT, POOL_DIM)[:, -1, skip:]
    pool_s = utail[NP // DEC_SEQ:, skip:]
    ckv_p = ckv_p.reshape(BATCH, SEQ, KV_LORA)
    ckv_s = ckv_s.reshape(DEC_BATCH, DEC_SEQ, KV_LORA)
    kpe_p = kpe_p.reshape(BATCH, SEQ, ROPE_DIM)
    kpe_s = kpe_s.reshape(DEC_BATCH, DEC_SEQ, ROPE_DIM)

    x = _ffn((x,), pre_up, pre_down, 1, ln_g[1, 0], ln_b[1, 0])
    nzx = SSM_INNER + CONV_DIM
    w_ssd = w_in_ssd.astype(BF16)
    zx, dtx = _ssd_in(x, w_ssd, jnp.pad(w_ssd[:, nzx:], ((0, 0), (0, LANES - SSM_HEADS))))
    conv_w8 = jnp.pad(conv_w, ((0, SUBLANES - CONV_WIDTH), (0, 0)))
    pad_lanes = lambda v, fill: jnp.pad(v.astype(F32), (0, LANES - v.shape[0]), constant_values=fill).reshape(1, LANES)
    a_neg = -jnp.exp(pad_lanes(a_log, 0.0))
    d_skip_x = jnp.repeat(d_skip.astype(F32), SSM_HEAD_DIM).reshape(1, SSM_INNER)
    onehot = (jnp.arange(LANES)[:, None] == (jnp.arange(SSM_INNER) // SSM_HEAD_DIM)[None, :]).astype(BF16)
    common = (conv_w8, conv_b.reshape(1, -1), pad_lanes(dt_bias, 0.0), a_neg, d_skip_x,
              ssm_norm_g.reshape(1, -1), onehot)
    hist = SUBLANES - (CONV_WIDTH - 1)
    y_p, h_p = _ssd_core(zx, dtx, 0, NP // SSD_L, SSD_L, False,
                         jnp.zeros((BATCH, SUBLANES, CONV_DIM), F32),
                         jnp.zeros((BATCH, SSM_STATE, SSM_INNER), F32), *common)
    y_s, h_s = _ssd_core(zx, dtx, NP, DEC_BATCH, DEC_SEQ, True,
                         jnp.pad(state_conv, ((0, 0), (hist, 0), (0, 0))),
                         _state_to_t(state_ssm), *common)
    x = _ssd_out(x, y_p, y_s, w_out_ssd, ln_g[1, 1], ln_b[1, 1])
    y_prompt, y_sample = _ffn((x,), post_up, post_down, 1, ln_g[1, 2], ln_b[1, 2], split_out=True)

    tail = CONV_WIDTH - 1
    conv_p = jnp.stack([zx[(b + 1) * SEQ - tail:(b + 1) * SEQ, SSM_INNER:] for b in range(BATCH)])
    conv_s = jnp.stack([zx[NP + (b + 1) * DEC_SEQ - tail:NP + (b + 1) * DEC_SEQ, SSM_INNER:]
                        for b in range(DEC_BATCH)])
    ssm_p = _state_from_t(h_p)
    ssm_s = _state_from_t(h_s)

    return (y_prompt.reshape(BATCH, SEQ, D_MODEL), y_sample.reshape(DEC_BATCH, DEC_SEQ, D_MODEL),
            pool_p, pool_s, ckv_p, ckv_s, kpe_p, kpe_s, conv_p, conv_s, ssm_p, ssm_s)
```

```python
import functools

import jax
import jax.numpy as jnp
import numpy as np
from jax import lax
from jax.experimental import pallas as pl
from jax.experimental.pallas import tpu as pltpu

F32 = jnp.float32
BF16 = jnp.bfloat16

D_MODEL = 2048
BATCH = 1
SEQ = 16384
DEPTH = 2
DEC_BATCH = 16
DEC_SEQ = 64
PAST_LEN = 1024
CHUNK = 64
ALPHA = (2 * DEPTH) ** 0.25
LN_EPS = 1e-5
RMS_EPS = 1e-6
FF_DIM = 5504
POOL_WINDOWS = (2, 4, 8, 16)
POOL_DIM = 512
POOL_GROUP = 128
POOL_STATE = 15
NOPE_DIM = 128
ROPE_DIM = 64
V_DIM = 128
MLA_HEADS = 12
Q_LORA = 512
KV_LORA = 512
ROPE_THETA = 10000.0
SSM_INNER = 4096
SSM_HEAD_DIM = 64
SSM_HEADS = 64
SSM_GROUPS = 8
SSM_HPG = 8
SSM_STATE = 128
CONV_WIDTH = 4
CONV_DIM = SSM_INNER + 2 * SSM_GROUPS * SSM_STATE

NP = BATCH * SEQ
NS = DEC_BATCH * DEC_SEQ
NTOK = NP + NS

LANES = 128
SUBLANES = 8
VMEM_LIMIT_BYTES = 56 * 1024 * 1024

FF_TILE = 512
FFN_TM = 512
FFN_WIDE_TM = 1024
FFN_WIDE_FF_TILE = 256
TOK_TM = 512
OUT_ROW_SPLIT = 2
ATT_TQ = 1024
ATT_TK = 512
SSD_L = 128
QK_PAD = 256
V_AUG = V_DIM + 2 * SUBLANES
SSD_IN_TM = 1024
SSD_IN_TN = 1024
POOL_EXT = 16
ATT_SCALE = float((NOPE_DIM + ROPE_DIM) ** -0.5)
ATT_SCALE_LOG2E = ATT_SCALE * 1.4426950408889634


def _cparams(sem):
    return pltpu.CompilerParams(dimension_semantics=sem, vmem_limit_bytes=VMEM_LIMIT_BYTES)


def _layer_norm(y, g, b):
    mu = jnp.mean(y, axis=-1, keepdims=True)
    d = y - mu
    var = jnp.mean(d * d, axis=-1, keepdims=True)
    return d * lax.rsqrt(var + LN_EPS) * g + b


def _rms(h, g):
    return h * lax.rsqrt(jnp.mean(h * h, axis=-1, keepdims=True) + RMS_EPS) * g


def _silu(x):
    h = 0.5 * x
    return h + h * jnp.tanh(h)


def _dot(a, b):
    return jnp.dot(a, b, preferred_element_type=F32)


def _dot_nt(a, b):
    return lax.dot_general(a, b, (((1,), (1,)), ((), ())), preferred_element_type=F32)


def _dot_tn(a, b):
    return lax.dot_general(a, b, (((0,), (0,)), ((), ())), preferred_element_type=F32)


def _split2(v):
    hi = v.astype(BF16)
    lo = (v - hi.astype(F32)).astype(BF16)
    return hi, lo


def _split3(v):
    hi = v.astype(BF16)
    r = v - hi.astype(F32)
    mid = r.astype(BF16)
    lo = (r - mid.astype(F32)).astype(BF16)
    return hi, mid, lo


def _ffn_kernel(npt, n_in, n_out, *refs):
    x_refs = refs[:n_in]
    wg_ref, wu_ref, wd_ref, wgt_ref, wut_ref, wdt_ref, g_ref, b_ref = refs[n_in:n_in + 8]
    o_refs = refs[n_in + 8:n_in + 8 + n_out]
    xb_ref = refs[n_in + 8 + n_out]
    acc_ref = o_refs[0] if n_out == 1 else refs[n_in + 9 + n_out]
    i = pl.program_id(0)
    j = pl.program_id(1)

    def load_x(rows=slice(None)):
        if n_in == 1:
            return x_refs[0][rows, :]
        return jnp.where(i < npt, x_refs[0][rows, :], x_refs[1][rows, :])

    def hidden(xb, wg, wu):
        return (_silu(_dot(xb, wg)) * _dot(xb, wu)).astype(BF16)

    last = pl.num_programs(1) - 1

    @pl.when(j == 0)
    def _():
        xb = load_x().astype(BF16)
        xb_ref[...] = xb
        acc_ref[...] = _dot(hidden(xb, wgt_ref[0], wut_ref[0]), wdt_ref[0])

    @pl.when(j < last)
    def _():
        acc_ref[...] += _dot(hidden(xb_ref[...], wg_ref[...], wu_ref[0]), wd_ref[...])

    @pl.when(j == last)
    def _():
        h = hidden(xb_ref[...], wg_ref[...], wu_ref[0])
        half = xb_ref.shape[0] // OUT_ROW_SPLIT
        for r in range(OUT_ROW_SPLIT):
            rows = pl.ds(r * half, half)
            acc = acc_ref[rows, :] + _dot(h[r * half:(r + 1) * half, :], wd_ref[...])
            y = _layer_norm(ALPHA * load_x(rows) + 0.5 * acc, g_ref[...], b_ref[...])
            if n_out == 1:
                o_refs[0][rows, :] = y
            else:
                @pl.when(i < npt)
                def _():
                    o_refs[0][rows, :] = y

                @pl.when(i >= npt)
                def _():
                    o_refs[1][rows, :] = y


def _ffn(xs, w_up_b, w_down_b, layer, g, b, split_out=False, wide=False):
    assert not (wide and (split_out or len(xs) != 1))
    tm, ff_tile = (FFN_WIDE_TM, FFN_WIDE_FF_TILE) if wide else (FFN_TM, FF_TILE)
    npt = NP // tm
    nff = FF_DIM // ff_tile
    tail0 = nff * ff_tile
    ff_tail = FF_DIM - tail0
    el = pl.Element
    aligned = lambda v: pl.multiple_of(v, LANES)
    n_in = len(xs)
    n_out = 2 if split_out else 1
    prompt_tile = lambda i, j: (jnp.minimum(i, npt - 1), 0)
    sample_tile = lambda i, j: (jnp.maximum(i - npt, 0), 0)
    whole_tile = lambda i, j: (i, 0)
    if wide:
        x_specs = [pl.BlockSpec((tm, D_MODEL), whole_tile, pipeline_mode=pl.Buffered(1))]
    else:
        x_specs = [pl.BlockSpec((tm, D_MODEL), m) for m in ((whole_tile,) if n_in == 1 else (prompt_tile, sample_tile))]
    if split_out:
        out_specs = [pl.BlockSpec((tm, D_MODEL), prompt_tile), pl.BlockSpec((tm, D_MODEL), sample_tile)]
        out_shape = [jax.ShapeDtypeStruct((NP, D_MODEL), F32), jax.ShapeDtypeStruct((NS, D_MODEL), F32)]
        scratch = [pltpu.VMEM((tm, D_MODEL), BF16), pltpu.VMEM((tm, D_MODEL), F32)]
    else:
        out_specs = pl.BlockSpec((tm, D_MODEL), whole_tile)
        out_shape = jax.ShapeDtypeStruct((NTOK, D_MODEL), F32)
        scratch = [pltpu.VMEM((tm, D_MODEL), BF16)]
    return pl.pallas_call(
        functools.partial(_ffn_kernel, npt, n_in, n_out),
        grid=(NTOK // tm, nff),
        in_specs=x_specs + [
            pl.BlockSpec((None, D_MODEL, ff_tile), lambda i, j: (layer, 0, j)),
            pl.BlockSpec((el(1), el(D_MODEL), el(ff_tile)), lambda i, j: (layer, 0, aligned(FF_DIM + j * ff_tile))),
            pl.BlockSpec((None, ff_tile, D_MODEL), lambda i, j: (layer, j, 0)),
            pl.BlockSpec((el(1), el(D_MODEL), el(ff_tail)), lambda i, j: (layer, 0, tail0), pipeline_mode=pl.Buffered(1)),
            pl.BlockSpec((el(1), el(D_MODEL), el(ff_tail)), lambda i, j: (layer, 0, FF_DIM + tail0), pipeline_mode=pl.Buffered(1)),
            pl.BlockSpec((el(1), el(ff_tail), el(D_MODEL)), lambda i, j: (layer, tail0, 0), pipeline_mode=pl.Buffered(1)),
            pl.BlockSpec((1, D_MODEL), lambda i, j: (0, 0)),
            pl.BlockSpec((1, D_MODEL), lambda i, j: (0, 0)),
        ],
        out_specs=out_specs,
        out_shape=out_shape,
        scratch_shapes=scratch,
        compiler_params=_cparams(("arbitrary", "arbitrary")),
        name="ffn_postnorm",
    )(*xs, w_up_b, w_up_b, w_down_b, w_up_b, w_up_b, w_down_b, g.reshape(1, -1), b.reshape(1, -1))


def _pool_window_mean_minus(ext_ref, base, rows, pos0):
    first = base + POOL_EXT
    t = lax.broadcasted_iota(jnp.int32, (rows, POOL_GROUP), 0)
    posp1 = (pos0 + t + 1).astype(F32)
    outs = []
    for g, w in enumerate(POOL_WINDOWS):
        cols = slice(g * POOL_GROUP, (g + 1) * POOL_GROUP)
        cur = ext_ref[pl.ds(first, rows), cols]
        tot = cur
        for s in range(1, w):
            tot = tot + ext_ref[pl.ds(first - s, rows), cols]
        outs.append(tot / jnp.minimum(posp1, float(w)) - cur)
    return outs


def _ab_in_kernel(n_prompt_tiles, x_ref, w_ref, past_ref, pw_ref, ps_ref, qg_ref, kg_ref, rope_ref,
                  ut_ref, yp_ref, qn_ref, ckvp_ref, ckvs_ref, kpep_ref, kpes_ref, ext_ref):
    i = pl.program_id(0)
    tm = x_ref.shape[0]
    h = _dot(x_ref[...].astype(BF16), w_ref[...])
    u = h[:, :POOL_DIM]
    for s in range(tm // DEC_SEQ):
        ut_ref[s] = u[(s + 1) * DEC_SEQ - POOL_EXT:(s + 1) * DEC_SEQ, :]
    qn_ref[...] = _rms(h[:, POOL_DIM:POOL_DIM + Q_LORA], qg_ref[...]).astype(BF16)
    ckv = _rms(h[:, POOL_DIM + Q_LORA:POOL_DIM + Q_LORA + KV_LORA], kg_ref[...])
    kv = h[:, POOL_DIM + Q_LORA + KV_LORA:] * rope_ref[...]
    kpe = (kv + pltpu.roll(kv, ROPE_DIM, axis=1))[:, :ROPE_DIM]

    @pl.when(i < n_prompt_tiles)
    def _():
        ckvp_ref[...] = ckv
        kpep_ref[...] = kpe

    @pl.when(i >= n_prompt_tiles)
    def _():
        ckvs_ref[...] = ckv
        kpes_ref[...] = kpe

    def finish(d_groups):
        for g in range(len(POOL_WINDOWS)):
            cols = slice(g * POOL_GROUP, (g + 1) * POOL_GROUP)
            y = _dot(d_groups[g].astype(BF16), pw_ref[g]) * ps_ref[:, cols]
            yp_ref[:, cols] = y.astype(BF16)

    @pl.when(i < n_prompt_tiles)
    def _():
        @pl.when(i == 0)
        def _():
            ext_ref[pl.ds(0, POOL_EXT), :] = jnp.zeros((POOL_EXT, POOL_DIM), F32)

        ext_ref[pl.ds(POOL_EXT, tm), :] = u
        finish(_pool_window_mean_minus(ext_ref, 0, tm, i * tm))
        ext_ref[pl.ds(0, POOL_EXT), :] = u[tm - POOL_EXT:, :]

    @pl.when(i >= n_prompt_tiles)
    def _():
        nseg = tm // DEC_SEQ
        stride = POOL_EXT + DEC_SEQ
        parts = [[] for _ in POOL_WINDOWS]
        for s in range(nseg):
            ext_ref[pl.ds(s * stride, POOL_EXT), :] = past_ref[s]
            ext_ref[pl.ds(s * stride + POOL_EXT, DEC_SEQ), :] = u[s * DEC_SEQ:(s + 1) * DEC_SEQ, :]
        for s in range(nseg):
            d = _pool_window_mean_minus(ext_ref, s * stride, DEC_SEQ, PAST_LEN)
            for g in range(len(POOL_WINDOWS)):
                parts[g].append(d[g])
        finish([jnp.concatenate(p, axis=0) for p in parts])


def _ab_in(x, w_in_x, pool_past, pool_w, pool_scale, q_norm_g, kv_norm_g, rope_k):
    n = x.shape[0]
    tm = TOK_TM
    nseg = tm // DEC_SEQ
    npt = NP // tm
    wcols = w_in_x.shape[1]
    ext_rows = max(tm + POOL_EXT, nseg * (POOL_EXT + DEC_SEQ))
    const = lambda i: (0, 0)
    prompt_tile = lambda i: (jnp.minimum(i, npt - 1), 0)
    sample_tile = lambda i: (jnp.maximum(i - npt, 0), 0)
    return pl.pallas_call(
        functools.partial(_ab_in_kernel, npt),
        grid=(n // tm,),
        in_specs=[
            pl.BlockSpec((tm, D_MODEL), lambda i: (i, 0)),
            pl.BlockSpec((D_MODEL, wcols), const),
            pl.BlockSpec((nseg, POOL_EXT, POOL_DIM), lambda i: (jnp.maximum(i - npt, 0), 0, 0)),
            pl.BlockSpec((len(POOL_WINDOWS), POOL_GROUP, POOL_GROUP), lambda i: (0, 0, 0)),
            pl.BlockSpec((1, POOL_DIM), const),
            pl.BlockSpec((1, Q_LORA), const),
            pl.BlockSpec((1, KV_LORA), const),
            pl.BlockSpec((tm, LANES), lambda i: (i, 0)),
        ],
        out_specs=[
            pl.BlockSpec((nseg, POOL_EXT, POOL_DIM), lambda i: (i, 0, 0)),
            pl.BlockSpec((tm, POOL_DIM), lambda i: (i, 0)),
            pl.BlockSpec((tm, Q_LORA), lambda i: (i, 0)),
            pl.BlockSpec((tm, KV_LORA), prompt_tile),
            pl.BlockSpec((tm, KV_LORA), sample_tile),
            pl.BlockSpec((tm, ROPE_DIM), prompt_tile),
            pl.BlockSpec((tm, ROPE_DIM), sample_tile),
        ],
        out_shape=[
            jax.ShapeDtypeStruct((n // DEC_SEQ, POOL_EXT, POOL_DIM), F32),
            jax.ShapeDtypeStruct((n, POOL_DIM), BF16),
            jax.ShapeDtypeStruct((n, Q_LORA), BF16),
            jax.ShapeDtypeStruct((NP, KV_LORA), F32),
            jax.ShapeDtypeStruct((NS, KV_LORA), F32),
            jax.ShapeDtypeStruct((NP, ROPE_DIM), F32),
            jax.ShapeDtypeStruct((NS, ROPE_DIM), F32),
        ],
        scratch_shapes=[pltpu.VMEM((ext_rows, POOL_DIM), F32)],
        compiler_params=_cparams(("arbitrary",)),
        name="ab_in",
    )(x, w_in_x, pool_past, pool_w.astype(BF16), pool_scale.reshape(1, -1),
      q_norm_g.reshape(1, -1), kv_norm_g.reshape(1, -1), rope_k)


def _q_up_kernel(qn_ref, w_ref, rope_ref, q_ref):
    qn = qn_ref[...]
    rope = rope_ref[...]
    lane = lax.broadcasted_iota(jnp.int32, rope.shape, 1)
    hpc = 4
    ncol = hpc * LANES
    for c in range(MLA_HEADS // hpc):
        nope = _dot(qn, w_ref[:, c * ncol:(c + 1) * ncol])
        rp = _dot(qn, w_ref[:, MLA_HEADS * LANES + c * ncol:MLA_HEADS * LANES + (c + 1) * ncol])
        for hh in range(hpc):
            hd = c * hpc + hh
            v = rp[:, hh * LANES:(hh + 1) * LANES] * rope
            rot = jnp.where(lane < ROPE_DIM, v + pltpu.roll(v, ROPE_DIM, axis=1), 0.0)
            q_ref[hd, :, 0:LANES] = (nope[:, hh * LANES:(hh + 1) * LANES] * ATT_SCALE).astype(BF16)
            q_ref[hd, :, LANES:QK_PAD] = (rot * ATT_SCALE).astype(BF16)


def _q_up(qn, wq_x, rope_q, row0, nrows):
    tm = TOK_TM
    t0 = row0 // tm
    return pl.pallas_call(
        _q_up_kernel,
        grid=(nrows // tm,),
        in_specs=[
            pl.BlockSpec((tm, Q_LORA), lambda i: (t0 + i, 0)),
            pl.BlockSpec(wq_x.shape, lambda i: (0, 0)),
            pl.BlockSpec((tm, LANES), lambda i: (t0 + i, 0)),
        ],
        out_specs=pl.BlockSpec((MLA_HEADS, tm, QK_PAD), lambda i: (0, i, 0)),
        out_shape=jax.ShapeDtypeStruct((MLA_HEADS, nrows, QK_PAD), BF16),
        compiler_params=_cparams(("parallel",)),
        name="q_up",
    )(qn, wq_x, rope_q)


def _q_up_t_kernel(qn_ref, wt_ref, ropet_ref, qt_ref):
    qt_all = _dot_nt(wt_ref[...], qn_ref[...])
    ropet = ropet_ref[...]
    nrope = MLA_HEADS * LANES
    zeros = jnp.zeros((QK_PAD - LANES - ROPE_DIM, qt_all.shape[1]), BF16)
    for hd in range(MLA_HEADS):
        v = qt_all[nrope + hd * LANES:nrope + (hd + 1) * LANES, :] * ropet
        rot = v[:ROPE_DIM] + v[ROPE_DIM:]
        qt_ref[hd, 0, 0:LANES, :] = (qt_all[hd * LANES:(hd + 1) * LANES, :] * ATT_SCALE_LOG2E).astype(BF16)
        qt_ref[hd, 0, LANES:LANES + ROPE_DIM, :] = (rot * ATT_SCALE_LOG2E).astype(BF16)
        qt_ref[hd, 0, LANES + ROPE_DIM:QK_PAD, :] = zeros


def _q_up_t(qn, wq_xt, rope_t, nrows):
    tm = ATT_TQ
    return pl.pallas_call(
        _q_up_t_kernel,
        grid=(nrows // tm,),
        in_specs=[
            pl.BlockSpec((tm, Q_LORA), lambda i: (i, 0)),
            pl.BlockSpec(wq_xt.shape, lambda i: (0, 0)),
            pl.BlockSpec((LANES, tm), lambda i: (0, i)),
        ],
        out_specs=pl.BlockSpec((MLA_HEADS, 1, QK_PAD, tm), lambda i: (0, i, 0, 0)),
        out_shape=jax.ShapeDtypeStruct((MLA_HEADS, nrows // tm, QK_PAD, tm), BF16),
        compiler_params=_cparams(("parallel",)),
        name="q_up_t",
    )(qn, wq_xt, rope_t)


def _kv_up_kernel(ckv_ref, kpe_ref, w_ref, k_ref, v_ref):
    c = ckv_ref[...].astype(BF16)
    kpe = kpe_ref[...].astype(BF16)
    hpc = 4
    ncol = hpc * LANES
    for j in range(MLA_HEADS // hpc):
        kn = _dot(c, w_ref[:, j * ncol:(j + 1) * ncol])
        vv = _dot(c, w_ref[:, MLA_HEADS * LANES + j * ncol:MLA_HEADS * LANES + (j + 1) * ncol])
        for hh in range(hpc):
            hd = j * hpc + hh
            k_ref[hd, :, 0:LANES] = kn[:, hh * LANES:(hh + 1) * LANES].astype(BF16)
            k_ref[hd, :, LANES:LANES + ROPE_DIM] = kpe
            k_ref[hd, :, LANES + ROPE_DIM:QK_PAD] = jnp.zeros_like(kpe)
            v_ref[hd] = vv[:, hh * LANES:(hh + 1) * LANES].astype(BF16)


def _kv_up(ckv, kpe, wkv_x):
    tm = TOK_TM
    nrows = ckv.shape[0]
    return pl.pallas_call(
        _kv_up_kernel,
        grid=(nrows // tm,),
        in_specs=[
            pl.BlockSpec((tm, KV_LORA), lambda i: (i, 0)),
            pl.BlockSpec((tm, ROPE_DIM), lambda i: (i, 0)),
            pl.BlockSpec(wkv_x.shape, lambda i: (0, 0)),
        ],
        out_specs=[
            pl.BlockSpec((MLA_HEADS, tm, QK_PAD), lambda i: (0, i, 0)),
            pl.BlockSpec((MLA_HEADS, tm, V_DIM), lambda i: (0, i, 0)),
        ],
        out_shape=[
            jax.ShapeDtypeStruct((MLA_HEADS, nrows, QK_PAD), BF16),
            jax.ShapeDtypeStruct((MLA_HEADS, nrows, V_DIM), BF16),
        ],
        compiler_params=_cparams(("parallel",)),
        name="kv_up",
    )(ckv, kpe, wkv_x)


def _kv_up_t_kernel(ckv_ref, kpe_ref, wk_ref, wvt_ref, k_ref, vt_ref):
    c = ckv_ref[...].astype(BF16)
    kpe = kpe_ref[...].astype(BF16)
    hpc = 4
    ncol = hpc * LANES
    for j in range(MLA_HEADS // hpc):
        kn = _dot(c, wk_ref[:, j * ncol:(j + 1) * ncol])
        for hh in range(hpc):
            hd = j * hpc + hh
            k_ref[hd, :, 0:LANES] = kn[:, hh * LANES:(hh + 1) * LANES].astype(BF16)
            k_ref[hd, :, LANES:LANES + ROPE_DIM] = kpe
            k_ref[hd, :, LANES + ROPE_DIM:QK_PAD] = jnp.zeros_like(kpe)
    vt_all = _dot_nt(wvt_ref[...], c)
    ones = jnp.ones((V_AUG - V_DIM, vt_all.shape[1]), BF16)
    for hd in range(MLA_HEADS):
        vt_ref[hd, 0, 0:V_DIM, :] = vt_all[hd * V_DIM:(hd + 1) * V_DIM, :].astype(BF16)
        vt_ref[hd, 0, V_DIM:V_AUG, :] = ones


def _kv_up_t(ckv, kpe, wk, wvt):
    tm = ATT_TK
    nrows = ckv.shape[0]
    return pl.pallas_call(
        _kv_up_t_kernel,
        grid=(nrows // tm,),
        in_specs=[
            pl.BlockSpec((tm, KV_LORA), lambda i: (i, 0)),
            pl.BlockSpec((tm, ROPE_DIM), lambda i: (i, 0)),
            pl.BlockSpec(wk.shape, lambda i: (0, 0)),
            pl.BlockSpec(wvt.shape, lambda i: (0, 0)),
        ],
        out_specs=[
            pl.BlockSpec((MLA_HEADS, tm, QK_PAD), lambda i: (0, i, 0)),
            pl.BlockSpec((MLA_HEADS, 1, V_AUG, tm), lambda i: (0, i, 0, 0)),
        ],
        out_shape=[
            jax.ShapeDtypeStruct((MLA_HEADS, nrows, QK_PAD), BF16),
            jax.ShapeDtypeStruct((MLA_HEADS, nrows // tm, V_AUG, tm), BF16),
        ],
        compiler_params=_cparams(("parallel",)),
        name="kv_up_t",
    )(ckv, kpe, wk, wvt)


def _attn_prompt_kernel(qt_ref, k_ref, vt_ref, o_ref, s0_ref, s1_ref, m_ref, acc_ref):
    i = pl.program_id(1)
    tk = s0_ref.shape[0]
    tq = qt_ref.shape[3]

    def scores(c, s_ref, lanes=slice(None)):
        start = pl.multiple_of(c * tk, tk)
        s_ref[:, lanes] = _dot(k_ref[0, pl.ds(start, tk), :], qt_ref[0, 0, :, lanes])

    def softmax_pv(c, s_ref, mask, lanes=slice(None)):
        st = s_ref[:, lanes]
        if mask is not None:
            st = jnp.where(mask, st, -jnp.inf)
        m = m_ref[:, lanes]
        m_new = jnp.maximum(m, jnp.max(st, axis=0, keepdims=True))
        alpha = jnp.exp2(m - m_new)
        p = jnp.exp2(st - m_new)
        m_ref[:, lanes] = m_new
        acc_ref[:, lanes] = alpha * acc_ref[:, lanes] + _dot(vt_ref[0, c], p.astype(BF16))

    m_ref[...] = jnp.full(m_ref.shape, -jnp.inf, F32)
    acc_ref[...] = jnp.zeros(acc_ref.shape, F32)
    key_chunk = lax.broadcasted_iota(jnp.int32, (tk, tq), 0) // CHUNK
    query_chunk = lax.broadcasted_iota(jnp.int32, (tk, tq), 1) // CHUNK
    diag = query_chunk >= key_chunk

    scores(0, s0_ref)

    def pair(j, carry):
        scores(2 * j + 1, s1_ref)
        softmax_pv(2 * j, s0_ref, None)
        scores(2 * j + 2, s0_ref)
        softmax_pv(2 * j + 1, s1_ref, None)
        return carry

    lax.fori_loop(0, i, pair, 0)
    upper = slice(tk, tq)
    scores(2 * i + 1, s1_ref, upper)
    softmax_pv(2 * i, s0_ref, diag)
    softmax_pv(2 * i + 1, s1_ref, diag[:, 0:tk], upper)

    o_ref[...] = (acc_ref[0:V_DIM, :] / acc_ref[V_DIM:V_DIM + 1, :]).T.astype(BF16)


def _attn_prompt(qt, k, vt):
    tq, tk = ATT_TQ, ATT_TK
    assert tq == 2 * tk and tk % CHUNK == 0
    score = pltpu.VMEM((tk, tq), F32)
    return pl.pallas_call(
        _attn_prompt_kernel,
        grid=(MLA_HEADS, NP // tq),
        in_specs=[
            pl.BlockSpec((1, 1, QK_PAD, tq), lambda h, i: (h, i, 0, 0)),
            pl.BlockSpec((1, NP, QK_PAD), lambda h, i: (h, 0, 0)),
            pl.BlockSpec((1, NP // tk, V_AUG, tk), lambda h, i: (h, 0, 0, 0)),
        ],
        out_specs=pl.BlockSpec((tq, V_DIM), lambda h, i: (i, h)),
        out_shape=jax.ShapeDtypeStruct((NP, MLA_HEADS * V_DIM), BF16),
        scratch_shapes=[score, score, pltpu.VMEM((1, tq), F32), pltpu.VMEM((V_AUG, tq), F32)],
        compiler_params=_cparams(("parallel", "arbitrary")),
        name="attn_prompt",
    )(qt, k, vt)


def _attn_sample_kernel(q_ref, kp_ref, vp_ref, kn_ref, vn_ref, o_ref):
    for hd in range(MLA_HEADS):
        q = q_ref[hd]
        sp = _dot_nt(q, kp_ref[hd])
        sn = _dot_nt(q, kn_ref[hd])
        m = jnp.maximum(jnp.max(sp, axis=1, keepdims=True), jnp.max(sn, axis=1, keepdims=True))
        pp = jnp.exp(sp - m)
        pn = jnp.exp(sn - m)
        l = jnp.sum(pp, axis=1, keepdims=True) + jnp.sum(pn, axis=1, keepdims=True)
        o = _dot(pp.astype(BF16), vp_ref[hd]) + _dot(pn.astype(BF16), vn_ref[hd])
        o_ref[:, hd * V_DIM:(hd + 1) * V_DIM] = (o / l).astype(BF16)


def _attn_sample(q, k_past, v_past, k_new, v_new):
    assert PAST_LEN % CHUNK == 0 and DEC_SEQ <= CHUNK
    return pl.pallas_call(
        _attn_sample_kernel,
        grid=(DEC_BATCH,),
        in_specs=[
            pl.BlockSpec((MLA_HEADS, DEC_SEQ, QK_PAD), lambda b: (0, b, 0)),
            pl.BlockSpec((MLA_HEADS, PAST_LEN, QK_PAD), lambda b: (0, b, 0)),
            pl.BlockSpec((MLA_HEADS, PAST_LEN, V_DIM), lambda b: (0, b, 0)),
            pl.BlockSpec((MLA_HEADS, DEC_SEQ, QK_PAD), lambda b: (0, b, 0)),
            pl.BlockSpec((MLA_HEADS, DEC_SEQ, V_DIM), lambda b: (0, b, 0)),
        ],
        out_specs=pl.BlockSpec((DEC_SEQ, MLA_HEADS * V_DIM), lambda b: (b, 0)),
        out_shape=jax.ShapeDtypeStruct((NS, MLA_HEADS * V_DIM), BF16),
        compiler_params=_cparams(("parallel",)),
        name="attn_sample",
    )(q, k_past, v_past, k_new, v_new)


def _ab_out_kernel(npt, x_ref, yp_ref, ap_ref, as_ref, wp_ref, wa_ref, g_ref, b_ref, o_ref):
    i = pl.program_id(0)
    half = x_ref.shape[0] // OUT_ROW_SPLIT
    for r in range(OUT_ROW_SPLIT):
        rows = pl.ds(r * half, half)
        att = jnp.where(i < npt, ap_ref[rows, :], as_ref[rows, :])
        mix = _dot(yp_ref[rows, :], wp_ref[...]) + _dot(att, wa_ref[...])
        o_ref[rows, :] = _layer_norm(ALPHA * x_ref[rows, :] + mix, g_ref[...], b_ref[...])


def _ab_out(x, ypool, att_p, att_s, w_out, g, b):
    n = x.shape[0]
    tm = TOK_TM
    npt = NP // tm
    w = w_out.astype(BF16)
    adim = MLA_HEADS * V_DIM
    const = lambda i: (0, 0)
    return pl.pallas_call(
        functools.partial(_ab_out_kernel, npt),
        grid=(n // tm,),
        in_specs=[
            pl.BlockSpec((tm, D_MODEL), lambda i: (i, 0)),
            pl.BlockSpec((tm, POOL_DIM), lambda i: (i, 0)),
            pl.BlockSpec((tm, adim), lambda i: (jnp.minimum(i, npt - 1), 0)),
            pl.BlockSpec((tm, adim), lambda i: (jnp.maximum(i - npt, 0), 0)),
            pl.BlockSpec((POOL_DIM, D_MODEL), const),
            pl.BlockSpec((adim, D_MODEL), const),
            pl.BlockSpec((1, D_MODEL), const),
            pl.BlockSpec((1, D_MODEL), const),
        ],
        out_specs=pl.BlockSpec((tm, D_MODEL), lambda i: (i, 0)),
        out_shape=jax.ShapeDtypeStruct((n, D_MODEL), F32),
        compiler_params=_cparams(("parallel",)),
        name="ab_out",
    )(x, ypool, att_p, att_s, w[:POOL_DIM], w[POOL_DIM:], g.reshape(1, -1), b.reshape(1, -1))


def _ssd_out_kernel(npt, x_ref, ap_ref, as_ref, w_ref, g_ref, b_ref, o_ref):
    i = pl.program_id(0)
    half = x_ref.shape[0] // OUT_ROW_SPLIT
    for r in range(OUT_ROW_SPLIT):
        rows = pl.ds(r * half, half)
        act = jnp.where(i < npt, ap_ref[rows, :], as_ref[rows, :])
        o_ref[rows, :] = _layer_norm(ALPHA * x_ref[rows, :] + _dot(act, w_ref[...]), g_ref[...], b_ref[...])


def _ssd_out(x, act_p, act_s, w_out, g, b):
    n = x.shape[0]
    tm = TOK_TM
    npt = NP // tm
    const = lambda i: (0, 0)
    return pl.pallas_call(
        functools.partial(_ssd_out_kernel, npt),
        grid=(n // tm,),
        in_specs=[
            pl.BlockSpec((tm, D_MODEL), lambda i: (i, 0)),
            pl.BlockSpec((tm, SSM_INNER), lambda i: (jnp.minimum(i, npt - 1), 0)),
            pl.BlockSpec((tm, SSM_INNER), lambda i: (jnp.maximum(i - npt, 0), 0)),
            pl.BlockSpec((SSM_INNER, D_MODEL), const),
            pl.BlockSpec((1, D_MODEL), const),
            pl.BlockSpec((1, D_MODEL), const),
        ],
        out_specs=pl.BlockSpec((tm, D_MODEL), lambda i: (i, 0)),
        out_shape=jax.ShapeDtypeStruct((n, D_MODEL), F32),
        compiler_params=_cparams(("parallel",)),
        name="ssd_out",
    )(x, act_p, act_s, w_out.astype(BF16), g.reshape(1, -1), b.reshape(1, -1))


def _ssd_in_kernel(x_ref, w_ref, wdt_ref, o_ref, dt_ref, xb_ref):
    @pl.when(pl.program_id(1) == 0)
    def _():
        xb_ref[...] = x_ref[...].astype(BF16)
        dt_ref[...] = _dot(xb_ref[...], wdt_ref[...])

    o_ref[...] = _dot(xb_ref[...], w_ref[...])


def _ssd_in(x, w_all, w_dt):
    n = x.shape[0]
    tm = SSD_IN_TM
    ncol = SSM_INNER + CONV_DIM
    w_zx = w_all
    return pl.pallas_call(
        _ssd_in_kernel,
        grid=(n // tm, ncol // SSD_IN_TN),
        in_specs=[
            pl.BlockSpec((tm, D_MODEL), lambda i, j: (i, 0)),
            pl.BlockSpec((D_MODEL, SSD_IN_TN), lambda i, j: (0, j)),
            pl.BlockSpec((D_MODEL, LANES), lambda i, j: (0, 0)),
        ],
        out_specs=[
            pl.BlockSpec((tm, SSD_IN_TN), lambda i, j: (i, j)),
            pl.BlockSpec((tm, LANES), lambda i, j: (i, 0)),
        ],
        out_shape=[
            jax.ShapeDtypeStruct((n, ncol), F32),
            jax.ShapeDtypeStruct((n, LANES), F32),
        ],
        scratch_shapes=[pltpu.VMEM((tm, D_MODEL), BF16)],
        compiler_params=_cparams(("parallel", "arbitrary")),
        name="ssd_in",
    )(x, w_zx, w_dt)


def _ssd_kernel(rows, fresh_each_step, z_ref, xs_ref, bc_ref, dt_ref, cpast_ref, h0_ref, cw_ref,
                cb_ref, dtb_ref, a_ref, dsk_ref, ng_ref, e_ref, y_ref, h_ref, ext_ref):
    c = pl.program_id(0)
    L = SSD_L
    G, R, P, S = SSM_GROUPS, SSM_HPG, SSM_HEAD_DIM, SSM_STATE
    halo = SUBLANES

    def start_sequence():
        ext_ref[pl.ds(0, halo), :] = cpast_ref[0]
        h_ref[0] = h0_ref[0]

    if fresh_each_step:
        start_sequence()
    else:
        pl.when(c == 0)(start_sequence)

    ext_ref[pl.ds(halo, rows), 0:SSM_INNER] = xs_ref[...]
    ext_ref[pl.ds(halo, rows), SSM_INNER:CONV_DIM] = bc_ref[...]
    if rows < L:
        ext_ref[pl.ds(halo + rows, L - rows), :] = jnp.zeros((L - rows, CONV_DIM), F32)

    assert CONV_WIDTH == 4
    win = ext_ref[pl.ds(0, halo + L), :]
    prev = pltpu.roll(win, 1, axis=0)
    near = cw_ref[3:4, :] * win[halo:] + cw_ref[2:3, :] * prev[halo:]
    far = cw_ref[1:2, :] * win + cw_ref[0:1, :] * prev
    conv = cb_ref[...] + near + pltpu.roll(far, 2, axis=0)[halo:]
    ext_ref[pl.ds(0, halo), :] = ext_ref[pl.ds(rows, halo), :]
    act = _silu(conv)
    xs = act[:, :SSM_INNER]
    xsb = xs.astype(BF16)
    bmat = act[:, SSM_INNER:SSM_INNER + G * S].astype(BF16)
    cmat = act[:, SSM_INNER + G * S:].astype(BF16)

    dtr = dt_ref[...] + dtb_ref[...]
    dt = jnp.maximum(dtr, 0.0) + jnp.log1p(jnp.exp(-jnp.abs(dtr)))
    if rows < L:
        dt = jnp.concatenate([dt, jnp.zeros((L - rows, LANES), F32)], axis=0)
    da = dt * a_ref[...]
    ti = lax.broadcasted_iota(jnp.int32, (L, L), 0)
    si = lax.broadcasted_iota(jnp.int32, (L, L), 1)
    causal = ti >= si
    tri = jnp.where(causal, 1.0, 0.0).astype(BF16)
    acs = sum(_dot(tri, part) for part in _split3(da))
    acs_t = acs.T
    dt_t = dt.T
    last = acs[L - 1:L, :]
    to_end = jnp.exp(last - acs) * dt
    eacs = jnp.exp(acs)
    onehot = e_ref[...]
    expand = lambda v: sum(_dot(part, onehot) for part in _split2(v))
    x_end = (xs * expand(to_end)).astype(BF16)
    eacs_x = expand(eacs)
    dec_x = sum(_dot(part, onehot) for part in _split3(jnp.broadcast_to(jnp.exp(last), (SUBLANES, LANES))))[0:1, :]

    lane4 = lax.broadcasted_iota(jnp.int32, (L, 4 * P), 1) // P
    for g in range(G):
        bg = bmat[:, g * S:(g + 1) * S]
        cg = cmat[:, g * S:(g + 1) * S]
        cbm = _dot_nt(cg, bg)
        cols = slice(g * R * P, (g + 1) * R * P)
        ht = h_ref[0, :, cols]
        y_state = _dot(cg, ht.astype(BF16)) * eacs_x[:, cols]
        y_parts = []
        for half in range(R // 4):
            x4 = xsb[:, g * R * P + half * 4 * P:g * R * P + (half + 1) * 4 * P]
            x_bd = jnp.concatenate([jnp.where(lane4 == r, x4, jnp.zeros_like(x4)) for r in range(4)], axis=0)
            wts = []
            for r in range(4):
                hd = g * R + half * 4 + r
                seg = acs[:, hd:hd + 1] - acs_t[hd:hd + 1, :]
                decay = jnp.exp(jnp.where(causal, seg, -jnp.inf))
                wts.append((cbm * decay * dt_t[hd:hd + 1, :]).astype(BF16))
            y_parts.append(_dot(jnp.concatenate(wts, axis=1), x_bd))
        y = jnp.concatenate(y_parts, axis=1) + y_state + dsk_ref[:, cols] * xs[:, cols]
        gz = y[:rows] * _silu(z_ref[:, cols])
        gz = gz * lax.rsqrt(jnp.mean(gz * gz, axis=-1, keepdims=True) + RMS_EPS)
        y_ref[:, cols] = (gz * ng_ref[:, cols]).astype(BF16)
        h_ref[0, :, cols] = ht * dec_x[:, cols] + _dot_tn(bg, x_end[:, cols])


def _ssd_core(zx, dtx, row0, nblk, rows, fresh_each_step, conv_past8, h0_t, conv_w8, conv_b, dt_bias, a_neg,
              d_skip_x, norm_g, onehot):
    L = SSD_L
    r0 = row0 // rows
    nseq = nblk if fresh_each_step else 1
    seq = (lambda c: c) if fresh_each_step else (lambda c: 0)
    const = lambda c: (0, 0)
    return pl.pallas_call(
        functools.partial(_ssd_kernel, rows, fresh_each_step),
        grid=(nblk,),
        in_specs=[
            pl.BlockSpec((rows, SSM_INNER), lambda c: (r0 + c, 0)),
            pl.BlockSpec((rows, SSM_INNER), lambda c: (r0 + c, 1)),
            pl.BlockSpec((rows, 2 * SSM_GROUPS * SSM_STATE), lambda c: (r0 + c, 4)),
            pl.BlockSpec((rows, LANES), lambda c: (r0 + c, 0)),
            pl.BlockSpec((1, SUBLANES, CONV_DIM), lambda c: (seq(c), 0, 0)),
            pl.BlockSpec((1, SSM_STATE, SSM_INNER), lambda c: (seq(c), 0, 0)),
            pl.BlockSpec((SUBLANES, CONV_DIM), const),
            pl.BlockSpec((1, CONV_DIM), const),
            pl.BlockSpec((1, LANES), const),
            pl.BlockSpec((1, LANES), const),
            pl.BlockSpec((1, SSM_INNER), const),
            pl.BlockSpec((1, SSM_INNER), const),
            pl.BlockSpec((LANES, SSM_INNER), const),
        ],
        out_specs=[
            pl.BlockSpec((rows, SSM_INNER), lambda c: (c, 0)),
            pl.BlockSpec((1, SSM_STATE, SSM_INNER), lambda c: (seq(c), 0, 0)),
        ],
        out_shape=[
            jax.ShapeDtypeStruct((nblk * rows, SSM_INNER), BF16),
            jax.ShapeDtypeStruct((nseq, SSM_STATE, SSM_INNER), F32),
        ],
        scratch_shapes=[pltpu.VMEM((L + 2 * SUBLANES, CONV_DIM), F32)],
        compiler_params=_cparams(("arbitrary",)),
        name="ssd_core_seq" if not fresh_each_step else "ssd_core_blocks",
    )(zx, zx, zx, dtx, conv_past8, h0_t, conv_w8, conv_b, dt_bias, a_neg, d_skip_x, norm_g, onehot)


def _rope_table():
    half = ROPE_DIM // 2
    inv = (1.0 / (np.float32(ROPE_THETA) ** (np.arange(half, dtype=np.float32) * np.float32(2.0) / np.float32(ROPE_DIM)))).astype(np.float32)
    pos = np.concatenate([
        np.tile(np.arange(SEQ, dtype=np.float32), BATCH),
        np.tile(np.arange(DEC_SEQ, dtype=np.float32) + np.float32(PAST_LEN), DEC_BATCH)])
    ang = (pos[:, None] * inv[None, :]).astype(np.float32).astype(np.float64)
    cos, sin = np.cos(ang), np.sin(ang)
    return np.concatenate([cos, cos, -sin, sin], axis=1).astype(np.float32)


def _swap_halves(w):
    half = w.shape[-1] // 2
    return jnp.concatenate([w[..., half:], w[..., :half]], axis=-1)


def _prep_ab_weights(w_in_ab, w_q_up, w_kv_up):
    kpe_w = w_in_ab[:, POOL_DIM + Q_LORA + KV_LORA:]
    w_in_x = jnp.concatenate([w_in_ab, _swap_halves(kpe_w)], axis=1).astype(BF16)
    wq = w_q_up.reshape(Q_LORA, MLA_HEADS, NOPE_DIM + ROPE_DIM)
    wq_rope = wq[..., NOPE_DIM:]
    wq_x = jnp.concatenate([
        wq[..., :NOPE_DIM].reshape(Q_LORA, -1),
        jnp.concatenate([wq_rope, _swap_halves(wq_rope)], axis=-1).reshape(Q_LORA, -1)], axis=1).astype(BF16)
    wkv = w_kv_up.reshape(KV_LORA, MLA_HEADS, NOPE_DIM + V_DIM)
    wk = wkv[..., :NOPE_DIM].reshape(KV_LORA, -1).astype(BF16)
    wv = wkv[..., NOPE_DIM:].reshape(KV_LORA, -1).astype(BF16)
    return w_in_x, wq_x, wk, wv


def _state_to_t(h):
    b = h.shape[0]
    return jnp.transpose(h, (0, 3, 1, 2)).reshape(b, SSM_STATE, SSM_INNER)


def _state_from_t(ht):
    b = ht.shape[0]
    return jnp.transpose(ht.reshape(b, SSM_STATE, SSM_HEADS, SSM_HEAD_DIM), (0, 2, 3, 1))


def kernel(x_prompt, x_sample, state_pool, cache_ckv, cache_kpe, state_conv, state_ssm, ffn_pre_up, ffn_pre_down, ffn_post_up, ffn_post_down, ln_g, ln_b, w_in_ab, pool_w, pool_scale, q_norm_g, w_q_up, kv_norm_g, w_kv_up, w_out_ab, w_in_ssd, conv_w, conv_b, dt_bias, a_log, d_skip, ssm_norm_g, w_out_ssd):
    pre_up, pre_down = ffn_pre_up.astype(BF16), ffn_pre_down.astype(BF16)
    post_up, post_down = ffn_post_up.astype(BF16), ffn_post_down.astype(BF16)

    x = _ffn((x_prompt.reshape(NP, D_MODEL), x_sample.reshape(NS, D_MODEL)), pre_up, pre_down, 0,
             ln_g[0, 0], ln_b[0, 0])
    rope = _rope_table()
    w_in_x, wq_x, wk, wv = _prep_ab_weights(w_in_ab, w_q_up, w_kv_up)
    wkv_x = jnp.concatenate([wk, wv], axis=1)
    pool_past = jnp.pad(state_pool, ((0, 0), (POOL_EXT - POOL_STATE, 0), (0, 0)))
    utail, ypool, qn, ckv_p, ckv_s, kpe_p, kpe_s = _ab_in(
        x, w_in_x, pool_past, pool_w, pool_scale, q_norm_g, kv_norm_g, rope)
    qt_p = _q_up_t(qn, wq_x.T, np.ascontiguousarray(rope[:NP].T), NP)
    k_p, vt_p = _kv_up_t(ckv_p, kpe_p, wk, wv.T)
    q_s = _q_up(qn, wq_x, rope, NP, NS)
    k_s, v_s = _kv_up(ckv_s, kpe_s, wkv_x)
    k_past, v_past = _kv_up(cache_ckv.reshape(DEC_BATCH * PAST_LEN, KV_LORA),
                            cache_kpe.reshape(DEC_BATCH * PAST_LEN, ROPE_DIM), wkv_x)
    att_p = _attn_prompt(qt_p, k_p, vt_p)
    att_s = _attn_sample(q_s, k_past, v_past, k_s, v_s)
    x = _ab_out(x, ypool, att_p, att_s, w_out_ab, ln_g[0, 1], ln_b[0, 1])
    x = _ffn((x,), post_up, post_down, 0, ln_g[0, 2], ln_b[0, 2], wide=True)

    skip = POOL_EXT - POOL_STATE
    pool_p = utail[:NP // DEC_SEQ].reshape(BATCH, SEQ // DEC_SEQ, POOL_EXT, POOL_DIM)[:, -1, skip:]
    pool_s = utail[NP // DEC_SEQ:, skip:]
    ckv_p = ckv_p.reshape(BATCH, SEQ, KV_LORA)
    ckv_s = ckv_s.reshape(DEC_BATCH, DEC_SEQ, KV_LORA)
    kpe_p = kpe_p.reshape(BATCH, SEQ, ROPE_DIM)
    kpe_s = kpe_s.reshape(DEC_BATCH, DEC_SEQ, ROPE_DIM)

    x = _ffn((x,), pre_up, pre_down, 1, ln_g[1, 0], ln_b[1, 0], wide=True)
    nzx = SSM_INNER + CONV_DIM
    w_ssd = w_in_ssd.astype(BF16)
    zx, dtx = _ssd_in(x, w_ssd, jnp.pad(w_ssd[:, nzx:], ((0, 0), (0, LANES - SSM_HEADS))))
    conv_w8 = jnp.pad(conv_w, ((0, SUBLANES - CONV_WIDTH), (0, 0)))
    pad_lanes = lambda v, fill: jnp.pad(v.astype(F32), (0, LANES - v.shape[0]), constant_values=fill).reshape(1, LANES)
    a_neg = -jnp.exp(pad_lanes(a_log, 0.0))
    d_skip_x = jnp.repeat(d_skip.astype(F32), SSM_HEAD_DIM).reshape(1, SSM_INNER)
    onehot = (jnp.arange(LANES)[:, None] == (jnp.arange(SSM_INNER) // SSM_HEAD_DIM)[None, :]).astype(BF16)
    common = (conv_w8, conv_b.reshape(1, -1), pad_lanes(dt_bias, 0.0), a_neg, d_skip_x,
              ssm_norm_g.reshape(1, -1), onehot)
    hist = SUBLANES - (CONV_WIDTH - 1)
    y_p, h_p = _ssd_core(zx, dtx, 0, NP // SSD_L, SSD_L, False,
                         jnp.zeros((BATCH, SUBLANES, CONV_DIM), F32),
                         jnp.zeros((BATCH, SSM_STATE, SSM_INNER), F32), *common)
    y_s, h_s = _ssd_core(zx, dtx, NP, DEC_BATCH, DEC_SEQ, True,
                         jnp.pad(state_conv, ((0, 0), (hist, 0), (0, 0))),
                         _state_to_t(state_ssm), *common)
    x = _ssd_out(x, y_p, y_s, w_out_ssd, ln_g[1, 1], ln_b[1, 1])
    y_prompt, y_sample = _ffn((x,), post_up, post_down, 1, ln_g[1, 2], ln_b[1, 2], split_out=True)

    tail = CONV_WIDTH - 1
    conv_p = jnp.stack([zx[(b + 1) * SEQ - tail:(b + 1) * SEQ, SSM_INNER:] for b in range(BATCH)])
    conv_s = jnp.stack([zx[NP + (b + 1) * DEC_SEQ - tail:NP + (b + 1) * DEC_SEQ, SSM_INNER:]
                        for b in range(DEC_BATCH)])
    ssm_p = _state_from_t(h_p)
    ssm_s = _state_from_t(h_s)

    return (y_prompt.reshape(BATCH, SEQ, D_MODEL), y_sample.reshape(DEC_BATCH, DEC_SEQ, D_MODEL),
            pool_p, pool_s, ckv_p, ckv_s, kpe_p, kpe_s, conv_p, conv_s, ssm_p, ssm_s)
```
